```python
import jax, jax.numpy as jnp
from jax import lax
import numpy as np

D_MODEL = 2048
BATCH = 2
SEQ = 4096
DEPTH = 4

GRID_W = 64
CTX_LEN = 256
N_MIXERS = 3
CHUNK = 64
EPS = 1e-6
GATE_CLIP = 30.0

FFN_HIDDEN = -(-8 * D_MODEL // (3 * 256)) * 256

HG_DK = 128
HG_HEADS = D_MODEL // HG_DK
HG_KEY_W = HG_HEADS * HG_DK
HG_DV = D_MODEL // HG_HEADS
HG_VAL_W = HG_HEADS * HG_DV

GLA_HEADS = 4
GLA_KEY_W = D_MODEL // 2
GLA_VAL_W = D_MODEL
GLA_DK = GLA_KEY_W // GLA_HEADS
GLA_DV = GLA_VAL_W // GLA_HEADS
GLA_GATE_RANK = 16
GLA_GATE_NORM = 16.0

DN_DK = 128
DN_DV = 128
DN_QK_HEADS = D_MODEL // 128
DN_V_HEADS = 2 * DN_QK_HEADS
DN_KEY_W = DN_QK_HEADS * DN_DK
DN_VAL_W = DN_V_HEADS * DN_DV
DN_QKV_W = 2 * DN_KEY_W + DN_VAL_W
CONV_K = 5

N_HG = (DEPTH + N_MIXERS - 1) // N_MIXERS
N_GLA = (DEPTH + N_MIXERS - 2) // N_MIXERS
N_DN = DEPTH // N_MIXERS

kernel_name = "hybrid_bidir_hgrn2_gla_deltanet_dit"


def _rms_norm(t, w):
    tf = t.astype(jnp.float32)
    y = tf * lax.rsqrt(jnp.mean(tf * tf, axis=-1, keepdims=True) + EPS)
    return (y * w.astype(jnp.float32)).astype(t.dtype)


def _modulate(t, shift, scale):
    return t * (1.0 + scale) + shift


def _l2norm(t):
    return t * lax.rsqrt(jnp.sum(t * t, axis=-1, keepdims=True) + EPS)


def _heads(t, n):
    return jnp.swapaxes(t.reshape(t.shape[:-1] + (n, t.shape[-1] // n)), -3, -2)


def _merge_heads(t):
    t = jnp.swapaxes(t, -3, -2)
    return t.reshape(t.shape[:-2] + (-1,))


def _two_way(fwd, bwd):
    return jnp.stack([fwd, jnp.flip(bwd, axis=-2)])


def _join_ways(o):
    return o[0] + jnp.flip(o[1], axis=-2)


def _split_chunks(t):
    t = t.reshape(t.shape[:-2] + (t.shape[-2] // CHUNK, CHUNK, t.shape[-1]))
    return jnp.moveaxis(t, -3, 0)


def _merge_chunks(t):
    t = jnp.moveaxis(t, 0, -3)
    return t.reshape(t.shape[:-3] + (-1, t.shape[-1]))


def _masked_exp(mask, rel):
    return jnp.where(mask, jnp.exp(jnp.where(mask, rel, 0.0)), 0.0)


def gla_chunk_scan(q, k, v, log_g, s0):
    tri = jnp.tril(jnp.ones((CHUNK, CHUNK), dtype=bool))[:, :, None]

    def step(s, blk):
        qc, kc, vc, gc = blk
        b = jnp.cumsum(gc, axis=-2)
        dec = _masked_exp(tri, b[..., :, None, :] - b[..., None, :, :])
        scores = jnp.einsum('...td,...sd,...tsd->...ts', qc, kc, dec)
        o = (jnp.einsum('...ts,...sv->...tv', scores, vc)
             + jnp.einsum('...td,...dv->...tv', qc * jnp.exp(b), s))
        b_end = b[..., -1, :]
        s_new = (jnp.exp(b_end)[..., :, None] * s
                 + jnp.einsum('...sd,...sv->...dv', kc * jnp.exp(b_end[..., None, :] - b), vc))
        return s_new, o

    s_fin, o = lax.scan(step, s0, tuple(_split_chunks(t) for t in (q, k, v, log_g)))
    return _merge_chunks(o), s_fin


def delta_chunk_scan(q, k, v, beta, log_a, s0):
    strict = jnp.tril(jnp.ones((CHUNK, CHUNK), dtype=bool), -1)
    incl = jnp.tril(jnp.ones((CHUNK, CHUNK), dtype=bool))
    eye = jnp.eye(CHUNK, dtype=jnp.float32)
    dv = v.shape[-1]

    def step(s, blk):
        qc, kc, vc, bc, gc = blk
        g = jnp.cumsum(gc[..., 0], axis=-1)
        rel = g[..., :, None] - g[..., None, :]
        dec_strict = _masked_exp(strict, rel)
        dec_incl = _masked_exp(incl, rel)
        lower = bc * jnp.einsum('...td,...sd->...ts', kc, kc) * dec_strict
        eg = jnp.exp(g)[..., None]
        rhs = jnp.concatenate([bc * vc, bc * eg * kc], axis=-1)
        sol = lax.linalg.triangular_solve(eye + lower, rhs, left_side=True, lower=True,
                                          unit_diagonal=True)
        w = sol[..., :dv] - jnp.einsum('...tk,...kv->...tv', sol[..., dv:], s)
        qk = jnp.einsum('...td,...sd->...ts', qc, kc) * dec_incl
        o = eg * jnp.einsum('...tk,...kv->...tv', qc, s) + jnp.einsum('...ts,...sv->...tv', qk, w)
        s_new = (jnp.exp(g[..., -1])[..., None, None] * s
                 + jnp.einsum('...sk,...sv->...kv', kc * jnp.exp(g[..., -1:] - g)[..., None], w))
        return s_new, o

    s_fin, o = lax.scan(step, s0, tuple(_split_chunks(t) for t in (q, k, v, beta, log_a)))
    return _merge_chunks(o), s_fin


def _gated_out(o, gate, o_norm, w_o):
    o = _merge_heads(_rms_norm(o, o_norm)).astype(gate.dtype)
    return (o * jax.nn.silu(gate)) @ w_o


def _short_conv(t, w, on_grid):
    b, n, ch = t.shape
    if on_grid:
        rows = n // GRID_W
        t = t.reshape(b * rows, GRID_W, ch)
    y = lax.conv_general_dilated(t, w.astype(t.dtype)[:, None, :], (1,),
                                 [(CONV_K // 2, CONV_K // 2)],
                                 dimension_numbers=('NWC', 'WIO', 'NWC'),
                                 feature_group_count=ch)
    return y.reshape(b, n, ch)


def hgrn2_mixer(a_ctx, a_lat, lb, w_in, w_f, o_norm, w_o, ctx_out):
    f32 = jnp.float32

    def prep(a):
        q, v, g = jnp.split(a @ w_in, [HG_KEY_W, HG_KEY_W + HG_VAL_W], axis=-1)
        z = jnp.einsum('bld,wdk->wblk', a, w_f).astype(f32)
        z = jnp.clip(z, -GATE_CLIP, GATE_CLIP)
        log_f = jax.nn.log_sigmoid(z) + jnp.log1p(lb * jnp.exp(-z))
        k = _heads((1.0 - lb) * jax.nn.sigmoid(-z), HG_HEADS)
        log_f = _heads(log_f, HG_HEADS)
        q = _heads(jax.nn.silu(q.astype(f32)), HG_HEADS) * (HG_DK ** -0.5)
        v = _heads(v.astype(f32), HG_HEADS)
        return (_two_way(q, q), _two_way(k[0], k[1]), _two_way(v, v),
                _two_way(log_f[0], log_f[1])), g

    in_c, g_c = prep(a_ctx)
    in_l, g_l = prep(a_lat)
    s0 = jnp.zeros((2, a_lat.shape[0], HG_HEADS, HG_DK, HG_DV), f32)
    o_c, s_c = gla_chunk_scan(*in_c, s0)
    o_l, _ = gla_chunk_scan(*in_l, s_c)
    y_lat = _gated_out(_join_ways(o_l), g_l, o_norm, w_o)
    y_ctx = _gated_out(_join_ways(o_c), g_c, o_norm, w_o) if ctx_out else None
    return y_ctx, y_lat


def gla_mixer(a_ctx, a_lat, w_in, w_g1, w_g2, b_g, o_norm, w_o, ctx_out):
    f32 = jnp.float32

    def prep(a):
        q, k, v, r = jnp.split(a @ w_in, [GLA_KEY_W, 2 * GLA_KEY_W, 2 * GLA_KEY_W + GLA_VAL_W],
                               axis=-1)
        low = jnp.einsum('bld,wdr->wblr', a, w_g1)
        gl = jnp.einsum('wblr,wrk->wblk', low, w_g2) + b_g[:, None, None, :]
        log_a = _heads(jax.nn.log_sigmoid(gl.astype(f32)) / GLA_GATE_NORM, GLA_HEADS)
        q = _heads(q.astype(f32), GLA_HEADS) * (GLA_DK ** -0.5)
        k = _heads(k.astype(f32), GLA_HEADS)
        v = _heads(v.astype(f32), GLA_HEADS)
        return (_two_way(q, q), _two_way(k, k), _two_way(v, v),
                _two_way(log_a[0], log_a[1])), r

    in_c, r_c = prep(a_ctx)
    in_l, r_l = prep(a_lat)
    s0 = jnp.zeros((2, a_lat.shape[0], GLA_HEADS, GLA_DK, GLA_DV), f32)
    o_c, s_c = gla_chunk_scan(*in_c, s0)
    o_l, _ = gla_chunk_scan(*in_l, s_c)
    y_lat = _gated_out(_join_ways(o_l), r_l, o_norm, w_o)
    y_ctx = _gated_out(_join_ways(o_c), r_c, o_norm, w_o) if ctx_out else None
    return y_ctx, y_lat


def gated_deltanet_mixer(a_ctx, a_lat, w_in, conv_w, w_b, w_a, a_log, dt_bias, o_norm, w_o,
                         ctx_out):
    f32 = jnp.float32
    rep = DN_V_HEADS // DN_QK_HEADS

    def prep(a, on_grid):
        qkv, z = jnp.split(a @ w_in, [DN_QKV_W], axis=-1)
        qkv = jax.nn.silu(_short_conv(qkv, conv_w, on_grid)).astype(f32)
        q, k, v = jnp.split(qkv, [DN_KEY_W, 2 * DN_KEY_W], axis=-1)
        q = jnp.repeat(_l2norm(_heads(q, DN_QK_HEADS)), rep, axis=-3) * (DN_DK ** -0.5)
        k = jnp.repeat(_l2norm(_heads(k, DN_QK_HEADS)), rep, axis=-3)
        v = _heads(v, DN_V_HEADS)
        beta = jax.nn.sigmoid(jnp.einsum('bld,wdh->wblh', a, w_b).astype(f32))
        dt = jax.nn.softplus(jnp.einsum('bld,wdh->wblh', a, w_a).astype(f32)
                             + dt_bias.astype(f32)[:, None, None, :])
        log_decay = -jnp.exp(a_log.astype(f32))[:, None, None, :] * dt
        beta = jnp.swapaxes(beta, -1, -2)[..., None]
        log_decay = jnp.swapaxes(log_decay, -1, -2)[..., None]
        return (_two_way(q, q), _two_way(k, k), _two_way(v, v),
                _two_way(beta[0], beta[1]), _two_way(log_decay[0], log_decay[1])), z

    in_c, z_c = prep(a_ctx, False)
    in_l, z_l = prep(a_lat, True)
    s0 = jnp.zeros((2, a_lat.shape[0], DN_V_HEADS, DN_DK, DN_DV), f32)
    o_c, s_c = delta_chunk_scan(*in_c, s0)
    o_l, _ = delta_chunk_scan(*in_l, s_c)
    y_lat = _gated_out(_join_ways(o_l), z_l, o_norm, w_o)
    y_ctx = _gated_out(_join_ways(o_c), z_c, o_norm, w_o) if ctx_out else None
    return y_ctx, y_lat


def _swiglu(t, w_gate, w_up, w_down):
    return (jax.nn.silu(t @ w_gate) * (t @ w_up)) @ w_down


def setup_inputs(seed: int = 0) -> dict:
    key = jax.random.key(seed)
    ks = iter(jax.random.split(key, 40))
    f32 = jnp.float32

    def nrm(shape, fan_in=1, scale=1.0):
        return jax.random.normal(next(ks), shape, f32) * (scale * fan_in ** -0.5)

    def gain(shape):
        return 1.0 + 0.02 * jax.random.normal(next(ks), shape, f32)

    dt = jnp.exp(jax.random.uniform(next(ks), (N_DN, 2, DN_V_HEADS), f32,
                                    minval=np.log(1e-3), maxval=np.log(1e-1)))
    return {
        "x": nrm((BATCH, SEQ, D_MODEL)),
        "c": nrm((BATCH, D_MODEL)),
        "ctx": nrm((BATCH, CTX_LEN, D_MODEL)),
        "c_ctx": nrm((D_MODEL,)),
        "w_ada": nrm((DEPTH, D_MODEL, 6 * D_MODEL), D_MODEL, 0.5),
        "b_ada": nrm((DEPTH, 6 * D_MODEL), 1, 0.01),
        "norm_mix": gain((DEPTH, D_MODEL)),
        "norm_ffn": gain((DEPTH, D_MODEL)),
        "hg_lb_logits": nrm((DEPTH, HG_KEY_W), 1, 0.5),
        "hg_w_in": nrm((N_HG, D_MODEL, HG_KEY_W + 2 * HG_VAL_W), D_MODEL),
        "hg_w_f": nrm((N_HG, 2, D_MODEL, HG_KEY_W), D_MODEL),
        "hg_o_norm": gain((N_HG, HG_DV)),
        "hg_w_o": nrm((N_HG, HG_VAL_W, D_MODEL), HG_VAL_W),
        "gla_w_in": nrm((N_GLA, D_MODEL, 2 * GLA_KEY_W + 2 * GLA_VAL_W), D_MODEL),
        "gla_w_g1": nrm((N_GLA, 2, D_MODEL, GLA_GATE_RANK), D_MODEL),
        "gla_w_g2": nrm((N_GLA, 2, GLA_GATE_RANK, GLA_KEY_W), GLA_GATE_RANK),
        "gla_b_g": nrm((N_GLA, 2, GLA_KEY_W), 1, 0.1),
        "gla_o_norm": gain((N_GLA, GLA_DV)),
        "gla_w_o": nrm((N_GLA, GLA_VAL_W, D_MODEL), GLA_VAL_W),
        "dn_w_in": nrm((N_DN, D_MODEL, DN_QKV_W + DN_VAL_W), D_MODEL),
        "dn_conv": nrm((N_DN, CONV_K, DN_QKV_W), CONV_K),
        "dn_w_b": nrm((N_DN, 2, D_MODEL, DN_V_HEADS), D_MODEL),
        "dn_w_a": nrm((N_DN, 2, D_MODEL, DN_V_HEADS), D_MODEL),
        "dn_a_log": jnp.log(jax.random.uniform(next(ks), (N_DN, 2, DN_V_HEADS), f32,
                                               minval=1.0, maxval=16.0)),
        "dn_dt_bias": dt + jnp.log(-jnp.expm1(-dt)),
        "dn_o_norm": gain((N_DN, DN_DV)),
        "dn_w_o": nrm((N_DN, DN_VAL_W, D_MODEL), DN_VAL_W),
        "ffn_w_gate": nrm((DEPTH, D_MODEL, FFN_HIDDEN), D_MODEL),
        "ffn_w_up": nrm((DEPTH, D_MODEL, FFN_HIDDEN), D_MODEL),
        "ffn_w_down": nrm((DEPTH, FFN_HIDDEN, D_MODEL), FFN_HIDDEN),
        "final_norm": gain((D_MODEL,)),
    }


def reference(x, c, ctx, c_ctx, w_ada, b_ada, norm_mix, norm_ffn, hg_lb_logits, hg_w_in, hg_w_f,
              hg_o_norm, hg_w_o, gla_w_in, gla_w_g1, gla_w_g2, gla_b_g, gla_o_norm, gla_w_o,
              dn_w_in, dn_conv, dn_w_b, dn_w_a, dn_a_log, dn_dt_bias, dn_o_norm, dn_w_o,
              ffn_w_gate, ffn_w_up, ffn_w_down, final_norm):
    p = jax.nn.softmax(hg_lb_logits.astype(jnp.float32), axis=0)
    lower_bounds = jnp.cumsum(p, axis=0) - p[0]

    s_lat = jax.nn.silu(c)[:, None, :]
    s_ctx = jax.nn.silu(c_ctx)
    h_lat, h_ctx = x, ctx
    for i in range(DEPTH):
        keep_ctx = i < DEPTH - 1
        m_lat = jnp.split(s_lat @ w_ada[i] + b_ada[i], 6, axis=-1)
        m_ctx = jnp.split(s_ctx @ w_ada[i] + b_ada[i], 6, axis=-1)
        a_lat = _modulate(_rms_norm(h_lat, norm_mix[i]), m_lat[0], m_lat[1])
        a_ctx = _modulate(_rms_norm(h_ctx, norm_mix[i]), m_ctx[0], m_ctx[1])
        kind, j = i % N_MIXERS, i // N_MIXERS
        if kind == 0:
            y_ctx, y_lat = hgrn2_mixer(a_ctx, a_lat, lower_bounds[i], hg_w_in[j], hg_w_f[j],
                                       hg_o_norm[j], hg_w_o[j], keep_ctx)
        elif kind == 1:
            y_ctx, y_lat = gla_mixer(a_ctx, a_lat, gla_w_in[j], gla_w_g1[j], gla_w_g2[j],
                                     gla_b_g[j], gla_o_norm[j], gla_w_o[j], keep_ctx)
        else:
            y_ctx, y_lat = gated_deltanet_mixer(a_ctx, a_lat, dn_w_in[j], dn_conv[j], dn_w_b[j],
                                                dn_w_a[j], dn_a_log[j], dn_dt_bias[j],
                                                dn_o_norm[j], dn_w_o[j], keep_ctx)
        h_lat = h_lat + m_lat[2] * y_lat
        h_lat = h_lat + m_lat[5] * _swiglu(
            _modulate(_rms_norm(h_lat, norm_ffn[i]), m_lat[3], m_lat[4]),
            ffn_w_gate[i], ffn_w_up[i], ffn_w_down[i])
        if keep_ctx:
            h_ctx = h_ctx + m_ctx[2] * y_ctx
            h_ctx = h_ctx + m_ctx[5] * _swiglu(
                _modulate(_rms_norm(h_ctx, norm_ffn[i]), m_ctx[3], m_ctx[4]),
                ffn_w_gate[i], ffn_w_up[i], ffn_w_down[i])
    return _rms_norm(h_lat, final_norm)
```

```python
import functools

import jax
import jax.numpy as jnp
from jax import lax
from jax.experimental import pallas as pl
from jax.experimental.pallas import tpu as pltpu

F32 = jnp.float32
BF16 = jnp.bfloat16

EPS = 1e-6
GATE_CLIP = 30.0
CHUNK = 64
GRID_W = 64
CONV_K = 5
GLA_HEADS = 4
GLA_GATE_NORM = 16.0
HEAD_DIM = 128
DECAY_SAFE = 60.0

V7X_VMEM_LIMIT = 56 * 1024 * 1024
MOD_ROWS = 8


def _params(sem):
    return pltpu.CompilerParams(dimension_semantics=sem, vmem_limit_bytes=V7X_VMEM_LIMIT)


def _sigmoid(x):
    return 1.0 / (1.0 + jnp.exp(-x))


def _silu(x):
    return x * _sigmoid(x)


def _softplus(x):
    return jnp.maximum(x, 0.0) + jnp.log1p(jnp.exp(-jnp.abs(x)))


def _log_sigmoid(x):
    return -_softplus(-x)


def _row_tile(t, target):
    best = 8
    for d in range(8, min(t, target) + 1, 8):
        if t % d == 0:
            best = d
    return best


def _col_tile(n, target):
    for c in (target, 512, 256, 128):
        if c <= target and n % c == 0:
            return c
    return n


def _mod_rows(m_ref, i, tm, tiles_per_batch, ctx, nb):
    b = i // tiles_per_batch
    r0 = (i % tiles_per_batch) * tm
    lat = m_ref[pl.ds(b, 1), :]
    cx = m_ref[nb:nb + 1, :]
    rows = lax.broadcasted_iota(jnp.int32, (tm, 1), 0) + r0
    return jnp.where(rows < ctx, cx, lat)


def _ada_kernel(c_ref, w_ref, b_ref, o_ref):
    s = _silu(c_ref[...]).astype(BF16)
    o_ref[...] = jnp.dot(s, w_ref[...].astype(BF16), preferred_element_type=F32) + b_ref[...]


def _ada_call(cc, w_ada, b_ada):
    depth, d, n = w_ada.shape
    tn = _col_tile(n, 1024)
    return pl.pallas_call(
        _ada_kernel,
        out_shape=jax.ShapeDtypeStruct((depth, MOD_ROWS, n), F32),
        grid=(depth, n // tn),
        in_specs=[pl.BlockSpec((MOD_ROWS, d), lambda i, j: (0, 0)),
                  pl.BlockSpec((None, d, tn), lambda i, j: (i, 0, j)),
                  pl.BlockSpec((None, 1, tn), lambda i, j: (i, 0, j))],
        out_specs=pl.BlockSpec((None, MOD_ROWS, tn), lambda i, j: (i, 0, j)),
        compiler_params=_params(("parallel", "parallel")),
        name="ada_mods",
    )(cc, w_ada, b_ada.reshape(depth, 1, n))


def _normmod_kernel(h_ref, nw_ref, sh_ref, sc_ref, o_ref, *, tm, tpb, ctx, nb):
    i = pl.program_id(0)
    x = h_ref[...]
    y = x * lax.rsqrt(jnp.mean(x * x, axis=-1, keepdims=True) + EPS) * nw_ref[...]
    shift = _mod_rows(sh_ref, i, tm, tpb, ctx, nb)
    scale = _mod_rows(sc_ref, i, tm, tpb, ctx, nb)
    o_ref[...] = (y * (1.0 + scale) + shift).astype(BF16)


def _normmod_call(h, norm_w, mods, layer, g_shift, g_scale, dims):
    m, d = h.shape
    t, ctx, nb = dims
    tm = _row_tile(t, 544)
    kern = functools.partial(_normmod_kernel, tm=tm, tpb=t // tm, ctx=ctx, nb=nb)
    return pl.pallas_call(
        kern,
        out_shape=jax.ShapeDtypeStruct((m, d), BF16),
        grid=(m // tm,),
        in_specs=[pl.BlockSpec((tm, d), lambda i: (i, 0)),
                  pl.BlockSpec((None, 1, d), lambda i: (layer, 0, 0)),
                  pl.BlockSpec((None, MOD_ROWS, d), lambda i: (layer, 0, g_shift)),
                  pl.BlockSpec((None, MOD_ROWS, d), lambda i: (layer, 0, g_scale))],
        out_specs=pl.BlockSpec((tm, d), lambda i: (i, 0)),
        compiler_params=_params(("parallel",)),
        name="norm_modulate",
    )(h, norm_w.reshape(norm_w.shape[0], 1, d), mods, mods)


def _dot(x, w_ref):
    return jnp.dot(x, w_ref[...].astype(BF16), preferred_element_type=F32)


def _mm_tiles(t, k, n, n_w, out_bytes, tm_target=1088, tn_target=512):
    budget = 40 * 1024 * 1024
    tm = _row_tile(t, tm_target)
    while tm > 64:
        for tn in (tn_target, 256):
            need = 2 * tm * k * 2 + 2 * n_w * k * tn * 4 + 2 * tm * tn * out_bytes + n_w * tm * tn * 4
            if n % tn == 0 and need <= budget:
                return tm, tn
        tm = _row_tile(t, tm - 8)
    return tm, _col_tile(n, 128)


def _hg_in_kernel(x_ref, w_ref, o_ref, *, tn, q_cols, scale):
    acc = _dot(x_ref[...], w_ref)
    o_ref[...] = jnp.where(pl.program_id(1) * tn < q_cols, _silu(acc) * scale, acc)


def _gla_in_kernel(x_ref, w_ref, o_ref, *, tn, q_cols, scale):
    acc = _dot(x_ref[...], w_ref)
    o_ref[...] = jnp.where(pl.program_id(1) * tn < q_cols, acc * scale, acc)


def _plain_kernel(x_ref, w_ref, o_ref, *, tn):
    o_ref[...] = _dot(x_ref[...], w_ref)


def _proj_call(kern, a, w, layer_idx, t, name, q_cols=0):
    m, k = a.shape
    n = w.shape[-1]
    tm, tn = _mm_tiles(t, k, n, 1, 4)
    while q_cols % tn:
        tn //= 2
    return pl.pallas_call(
        functools.partial(kern, tn=tn),
        out_shape=jax.ShapeDtypeStruct((m, n), F32),
        grid=(m // tm, n // tn),
        in_specs=[pl.BlockSpec((tm, k), lambda i, j: (i, 0)),
                  pl.BlockSpec((None, k, tn), lambda i, j: (layer_idx, 0, j))],
        out_specs=pl.BlockSpec((tm, tn), lambda i, j: (i, j)),
        compiler_params=_params(("parallel", "arbitrary")),
        name=name,
    )(a, w)


def _hg_f_kernel(x_ref, w_ref, lb_ref, k_ref, lf_ref):
    z = jnp.clip(_dot(x_ref[...], w_ref), -GATE_CLIP, GATE_CLIP)
    lb = lb_ref[...]
    lf_ref[...] = _log_sigmoid(z) + jnp.log1p(lb * jnp.exp(-z))
    k_ref[...] = (1.0 - lb) * _sigmoid(-z)


def _hg_f_call(a, w_f, layer_idx, lb, t):
    m, k = a.shape
    n = w_f.shape[-1]
    tm, tn = _mm_tiles(t, k, n, 1, 8)
    nt = n // tn
    out = jax.ShapeDtypeStruct((2, m, n), F32)
    return pl.pallas_call(
        _hg_f_kernel,
        out_shape=(out, out),
        grid=(m // tm, 2 * nt),
        in_specs=[pl.BlockSpec((tm, k), lambda i, j: (i, 0)),
                  pl.BlockSpec((None, None, k, tn), lambda i, j: (layer_idx, j // nt, 0, j % nt)),
                  pl.BlockSpec((1, tn), lambda i, j: (0, j % nt))],
        out_specs=(pl.BlockSpec((None, tm, tn), lambda i, j: (j // nt, i, j % nt)),
                   pl.BlockSpec((None, tm, tn), lambda i, j: (j // nt, i, j % nt))),
        compiler_params=_params(("parallel", "arbitrary")),
        name="hgrn2_gates",
    )(a, w_f, lb.reshape(1, n))


def _gla_gate_kernel(x_ref, w1_ref, w2_ref, b_ref, o_ref):
    low = _dot(x_ref[...], w1_ref)
    gl = _dot(low.astype(BF16), w2_ref) + b_ref[...]
    o_ref[...] = _log_sigmoid(gl) * (1.0 / GLA_GATE_NORM)


def _gla_gate_call(a, w_g1, w_g2, b_g, layer_idx, t):
    m, k = a.shape
    rank, kw = w_g2.shape[-2:]
    tm = _row_tile(t, 544)
    return pl.pallas_call(
        _gla_gate_kernel,
        out_shape=jax.ShapeDtypeStruct((2, m, kw), F32),
        grid=(m // tm, 2),
        in_specs=[pl.BlockSpec((tm, k), lambda i, d: (i, 0)),
                  pl.BlockSpec((None, None, k, rank), lambda i, d: (layer_idx, d, 0, 0)),
                  pl.BlockSpec((None, None, rank, kw), lambda i, d: (layer_idx, d, 0, 0)),
                  pl.BlockSpec((None, None, 1, kw), lambda i, d: (layer_idx, d, 0, 0))],
        out_specs=pl.BlockSpec((None, tm, kw), lambda i, d: (d, i, 0)),
        compiler_params=_params(("parallel", "arbitrary")),
        name="gla_gates",
    )(a, w_g1, w_g2, b_g.reshape(b_g.shape[0], 2, 1, kw))


def _dn_bg_kernel(x_ref, wb_ref, wa_ref, alog_ref, dtb_ref, beta_ref, ld_ref):
    x = x_ref[...]
    beta_ref[...] = _sigmoid(_dot(x, wb_ref))
    dt = _softplus(_dot(x, wa_ref) + dtb_ref[...])
    ld_ref[...] = -jnp.exp(alog_ref[...]) * dt


def _dn_bg_call(a, w_b, w_a, a_log, dt_bias, layer_idx, t):
    m, k = a.shape
    hv = w_b.shape[-1]
    tm = _row_tile(t, 1088)
    out = jax.ShapeDtypeStruct((2, m, hv), F32)
    wspec = pl.BlockSpec((None, None, k, hv), lambda i, d: (layer_idx, d, 0, 0))
    pspec = pl.BlockSpec((None, None, 1, hv), lambda i, d: (layer_idx, d, 0, 0))
    ospec = pl.BlockSpec((None, tm, hv), lambda i, d: (d, i, 0))
    nl = a_log.shape[0]
    return pl.pallas_call(
        _dn_bg_kernel,
        out_shape=(out, out),
        grid=(m // tm, 2),
        in_specs=[pl.BlockSpec((tm, k), lambda i, d: (i, 0)), wspec, wspec, pspec, pspec],
        out_specs=(ospec, ospec),
        compiler_params=_params(("parallel", "arbitrary")),
        name="deltanet_beta_decay",
    )(a, w_b, w_a, a_log.reshape(nl, 2, 1, hv), dt_bias.reshape(nl, 2, 1, hv))


def _swiglu_kernel(x_ref, wg_ref, wu_ref, o_ref):
    x = x_ref[...]
    o_ref[...] = (_silu(_dot(x, wg_ref)) * _dot(x, wu_ref)).astype(BF16)


def _swiglu_call(a, w_gate, w_up, layer, t):
    m, k = a.shape
    n = w_gate.shape[-1]
    tm, tn = _mm_tiles(t, k, n, 2, 2)
    wspec = pl.BlockSpec((None, k, tn), lambda i, j: (layer, 0, j))
    return pl.pallas_call(
        _swiglu_kernel,
        out_shape=jax.ShapeDtypeStruct((m, n), BF16),
        grid=(m // tm, n // tn),
        in_specs=[pl.BlockSpec((tm, k), lambda i, j: (i, 0)), wspec, wspec],
        out_specs=pl.BlockSpec((tm, tn), lambda i, j: (i, j)),
        compiler_params=_params(("parallel", "arbitrary")),
        name="ffn_swiglu",
    )(a, w_gate, w_up)


def _residual_kernel(x_ref, w_ref, h_ref, g_ref, o_ref, *, tm, tpb, ctx, nb):
    gate = _mod_rows(g_ref, pl.program_id(0), tm, tpb, ctx, nb)
    o_ref[...] = h_ref[...] + gate * _dot(x_ref[...], w_ref)


def _residual_call(y, w, w_idx, h, mods, layer, g_gate, dims, name):
    m, k = y.shape
    d = h.shape[1]
    t, ctx, nb = dims
    tm, tn = _mm_tiles(t, k, d, 1, 12)
    nt = d // tn
    kern = functools.partial(_residual_kernel, tm=tm, tpb=t // tm, ctx=ctx, nb=nb)
    return pl.pallas_call(
        kern,
        out_shape=jax.ShapeDtypeStruct(h.shape, F32),
        grid=(m // tm, nt),
        in_specs=[pl.BlockSpec((tm, k), lambda i, j: (i, 0)),
                  pl.BlockSpec((None, k, tn), lambda i, j: (w_idx, 0, j)),
                  pl.BlockSpec((tm, tn), lambda i, j: (i, j)),
                  pl.BlockSpec((None, MOD_ROWS, tn), lambda i, j: (layer, 0, g_gate * nt + j))],
        out_specs=pl.BlockSpec((tm, tn), lambda i, j: (i, j)),
        input_output_aliases={2: 0},
        compiler_params=_params(("parallel", "arbitrary")),
        name=name,
    )(y, w, h, mods)


def _gated_norm_kernel(of_ref, ob_ref, g_ref, nw_ref, y_ref, *, dv, heads):
    nw = nw_ref[...]
    for hh in range(heads):
        cols = slice(hh * dv, (hh + 1) * dv)
        o = of_ref[:, cols] + ob_ref[:, cols]
        y = o * lax.rsqrt(jnp.mean(o * o, axis=-1, keepdims=True) + EPS) * nw
        y_ref[:, cols] = (y * _silu(g_ref[:, cols])).astype(BF16)


def _gated_norm_call(o_f, o_b, gate_arr, gate_col0, o_norm, norm_idx, t):
    m, width = o_f.shape
    dv = o_norm.shape[-1]
    tc = min(width, max(dv, 512))
    tm = _row_tile(t, 544)
    goff = gate_col0 // tc
    kern = functools.partial(_gated_norm_kernel, dv=dv, heads=tc // dv)
    return pl.pallas_call(
        kern,
        out_shape=jax.ShapeDtypeStruct((m, width), BF16),
        grid=(m // tm, width // tc),
        in_specs=[pl.BlockSpec((tm, tc), lambda i, j: (i, j)),
                  pl.BlockSpec((tm, tc), lambda i, j: (i, j)),
                  pl.BlockSpec((tm, tc), lambda i, j: (i, goff + j)),
                  pl.BlockSpec((None, 1, dv), lambda i, j: (norm_idx, 0, 0))],
        out_specs=pl.BlockSpec((tm, tc), lambda i, j: (i, j)),
        compiler_params=_params(("parallel", "parallel")),
        name="gated_head_norm",
    )(o_f, o_b, gate_arr, o_norm.reshape(o_norm.shape[0], 1, dv))


def _split3(x):
    x1 = x.astype(BF16)
    r1 = x - x1.astype(F32)
    x2 = r1.astype(BF16)
    x3 = (r1 - x2.astype(F32)).astype(BF16)
    return x1, x2, x3


def _bwd_group(s, ng):
    return jnp.where(s == 0, 0, ng - s)


def _dot_nt(a, b):
    return lax.dot_general(a, b, (((1,), (1,)), ((), ())), preferred_element_type=F32)


def _gla_scan_kernel(qf_ref, kf_ref, vf_ref, gf_ref, qb_ref, kb_ref, vb_ref, gb_ref,
                     of_ref, ob_ref, st_ref, b_ref, *, gt, dk, dv):
    c = CHUNK
    nch = gt // c

    @pl.when(pl.program_id(2) == 0)
    def _():
        st_ref[...] = jnp.zeros_like(st_ref)

    row = lax.broadcasted_iota(jnp.int32, (gt, gt), 0)
    col = lax.broadcasted_iota(jnp.int32, (gt, gt), 1)
    shift = c.bit_length() - 1
    same = jnp.right_shift(row, shift) == jnp.right_shift(col, shift)
    tri = (jnp.where(same & (col <= row), 1.0, 0.0).astype(BF16),
           jnp.where(same & (col >= row), 1.0, 0.0).astype(BF16))
    crow = lax.broadcasted_iota(jnp.int32, (c, c), 0)
    ccol = lax.broadcasted_iota(jnp.int32, (c, c), 1)
    masks = (ccol <= crow, ccol >= crow)
    mid = (c // 2 - 1, c // 2)
    end = (c - 1, 0)
    g_refs = (gf_ref, gb_ref)
    q_refs = (qf_ref, qb_ref)
    k_refs = (kf_ref, kb_ref)
    v_refs = (vf_ref, vb_ref)
    o_refs = (of_ref, ob_ref)

    worst = jnp.zeros((1, dk), F32)
    for d in range(2):
        g1, g2, g3 = _split3(g_refs[d][...])
        b = (jnp.dot(tri[d], g1, preferred_element_type=F32)
             + jnp.dot(tri[d], g2, preferred_element_type=F32)
             + jnp.dot(tri[d], g3, preferred_element_type=F32))
        b_ref[d] = b
        for ci in range(nch):
            r = b[ci * c + mid[d]:ci * c + mid[d] + 1, :]
            e = b[ci * c + end[d]:ci * c + end[d] + 1, :]
            worst = jnp.maximum(worst, jnp.maximum(-r, r - e))
    safe = jnp.max(worst) <= DECAY_SAFE

    def scores_fast(d, ci, q, k, b):
        r = b[mid[d]:mid[d] + 1, :]
        qs = (q * jnp.exp(b - r)).astype(BF16)
        ks = (k * jnp.exp(r - b)).astype(BF16)
        return jnp.where(masks[d], _dot_nt(qs, ks), 0.0)

    def scores_exact(d, ci, q, k, b):
        lane = lax.broadcasted_iota(jnp.int32, (c, c), 1)

        def body(s, acc):
            ks = k_refs[d][pl.ds(ci * c + s, 1), :]
            bs = b_ref[d, pl.ds(ci * c + s, 1), :]
            term = q * ks * jnp.exp(jnp.minimum(b - bs, 0.0))
            return jnp.where(lane == s, jnp.sum(term, axis=1, keepdims=True), acc)

        acc = lax.fori_loop(0, c, body, jnp.zeros((c, c), F32))
        return jnp.where(masks[d], acc, 0.0)

    def run(scores):
        for step in range(nch):
            for d in range(2):
                ci = step if d == 0 else nch - 1 - step
                rows = slice(ci * c, (ci + 1) * c)
                q = q_refs[d][rows, :]
                k = k_refs[d][rows, :]
                v = v_refs[d][rows, :].astype(BF16)
                b = b_ref[d, rows, :]
                b_end = b[end[d]:end[d] + 1, :]
                a = scores(d, ci, q, k, b).astype(BF16)
                qe = (q * jnp.exp(b)).astype(BF16)
                ke = (k * jnp.exp(b_end - b)).astype(BF16)
                st = st_ref[d]
                o = jnp.dot(a, v, preferred_element_type=F32) + _dot_nt(qe, st.astype(BF16))
                o_refs[d][rows, :] = o
                vt = v_refs[d][rows, :].T.astype(BF16)
                st_ref[d] = st * jnp.exp(b_end) + jnp.dot(vt, ke, preferred_element_type=F32)

    @pl.when(safe)
    def _():
        run(scores_fast)

    @pl.when(jnp.logical_not(safe))
    def _():
        run(scores_exact)


def _gla_scan_call(q_src, k_srcs, v_src, g_arr, nb, t, gt, heads, dk, dv):
    ng = t // gt

    def spec3(col0, width, bwd):
        off = col0 // width
        if bwd:
            return pl.BlockSpec((None, gt, width), lambda b, h, s: (b, _bwd_group(s, ng), off + h))
        return pl.BlockSpec((None, gt, width), lambda b, h, s: (b, s, off + h))

    def spec4(lead, col0, width, bwd):
        off = col0 // width
        if bwd:
            return pl.BlockSpec((None, None, gt, width),
                                lambda b, h, s: (lead, b, _bwd_group(s, ng), off + h))
        return pl.BlockSpec((None, None, gt, width), lambda b, h, s: (lead, b, s, off + h))

    def kspec(src, bwd):
        arr, lead, col0 = src
        return spec3(col0, dk, bwd) if lead is None else spec4(lead, col0, dk, bwd)

    in_specs, args = [], []
    for d in range(2):
        bwd = d == 1
        in_specs += [spec3(q_src[1], dk, bwd), kspec(k_srcs[d], bwd), spec3(v_src[1], dv, bwd),
                     spec4(d, 0, dk, bwd)]
        args += [q_src[0], k_srcs[d][0], v_src[0], g_arr]
    out = jax.ShapeDtypeStruct((nb, t, heads * dv), F32)
    kern = functools.partial(_gla_scan_kernel, gt=gt, dk=dk, dv=dv)
    return pl.pallas_call(
        kern,
        out_shape=(out, out),
        grid=(nb, heads, ng),
        in_specs=in_specs,
        out_specs=(pl.BlockSpec((None, gt, dv), lambda b, h, s: (b, s, h)),
                   pl.BlockSpec((None, gt, dv), lambda b, h, s: (b, _bwd_group(s, ng), h))),
        scratch_shapes=[pltpu.VMEM((2, dv, dk), F32), pltpu.VMEM((2, gt, dk), F32)],
        compiler_params=_params(("parallel", "parallel", "arbitrary")),
        name="gla_scan",
    )(*args)


def _dn_conv_kernel(x_ref, w_ref, o_ref, *, gt, n_norm_tiles, n_q_tiles, qscale):
    s = pl.program_id(1)
    j = pl.program_id(2)
    x = x_ref[...]
    w = w_ref[...]
    seg = jnp.where(s == 0, gt, GRID_W)
    r = lax.broadcasted_iota(jnp.int32, (gt, 1), 0)
    pos = jnp.where(s == 0, r, jnp.bitwise_and(r, GRID_W - 1))
    acc = x * w[CONV_K // 2:CONV_K // 2 + 1, :]
    for tap in range(CONV_K):
        off = tap - CONV_K // 2
        if off == 0:
            continue
        shifted = pltpu.roll(x, (-off) % gt, 0)
        ok = (pos + off >= 0) & (pos + off < seg)
        acc = acc + jnp.where(ok, shifted, 0.0) * w[tap:tap + 1, :]
    y = _silu(acc)
    o_ref[...] = y

    @pl.when(j < n_norm_tiles)
    def _():
        scale = jnp.where(j < n_q_tiles, qscale, 1.0)
        for hh in range(y.shape[1] // HEAD_DIM):
            cols = slice(hh * HEAD_DIM, (hh + 1) * HEAD_DIM)
            yh = y[:, cols]
            o_ref[:, cols] = yh * (lax.rsqrt(jnp.sum(yh * yh, axis=-1, keepdims=True) + EPS) * scale)


def _dn_conv_call(p, conv_w, layer_idx, nb, t, gt, qkv_w, key_w):
    tc = min(512, key_w)
    kern = functools.partial(_dn_conv_kernel, gt=gt, n_norm_tiles=2 * key_w // tc,
                             n_q_tiles=key_w // tc, qscale=HEAD_DIM ** -0.5)
    return pl.pallas_call(
        kern,
        out_shape=jax.ShapeDtypeStruct((nb, t, qkv_w), F32),
        grid=(nb, t // gt, qkv_w // tc),
        in_specs=[pl.BlockSpec((None, gt, tc), lambda b, s, j: (b, s, j)),
                  pl.BlockSpec((None, CONV_K, tc), lambda b, s, j: (layer_idx, 0, j))],
        out_specs=pl.BlockSpec((None, gt, tc), lambda b, s, j: (b, s, j)),
        compiler_params=_params(("parallel", "parallel", "parallel")),
        name="deltanet_conv",
    )(p, conv_w)


def _mm3(a, b):
    a1 = a.astype(BF16)
    a2 = (a - a1.astype(F32)).astype(BF16)
    b1 = b.astype(BF16)
    b2 = (b - b1.astype(F32)).astype(BF16)
    return (jnp.dot(a1, b1, preferred_element_type=F32) + jnp.dot(a1, b2, preferred_element_type=F32)
            + jnp.dot(a2, b1, preferred_element_type=F32))


def _unit_lower_inverse(n, eye):
    p = eye - n
    x = n
    levels = CHUNK.bit_length() - 2
    for _ in range(levels):
        x = _mm3(x, x)
        p = p + _mm3(p, x)
    return p


def _dn_scan_kernel(qf_ref, kf_ref, vf_ref, qb_ref, kb_ref, vb_ref, beta_f_ref, ld_f_ref,
                    beta_b_ref, ld_b_ref, of_ref, ob_ref, st_ref, *, gt, rep):
    c = CHUNK
    nch = gt // c

    @pl.when(pl.program_id(2) == 0)
    def _():
        st_ref[...] = jnp.zeros_like(st_ref)

    row = lax.broadcasted_iota(jnp.int32, (c, c), 0)
    col = lax.broadcasted_iota(jnp.int32, (c, c), 1)
    diag = row == col
    eye = jnp.where(diag, 1.0, 0.0)
    incl = (col <= row, col >= row)
    strict = (col < row, col > row)
    end = (c - 1, 0)
    q_refs = (qf_ref, qb_ref)
    k_refs = (kf_ref, kb_ref)
    v_refs = (vf_ref, vb_ref)
    o_refs = (of_ref, ob_ref)
    beta_refs = (beta_f_ref, beta_b_ref)
    ld_refs = (ld_f_ref, ld_b_ref)

    def to_col(x_row):
        return jnp.sum(jnp.where(diag, x_row, 0.0), axis=1, keepdims=True)

    for step in range(nch):
        for d in range(2):
            ci = step if d == 0 else nch - 1 - step
            rows = slice(ci * c, (ci + 1) * c)
            q = q_refs[d][rows, :]
            k = k_refs[d][rows, :]
            qb16 = q.astype(BF16)
            kb16 = k.astype(BF16)
            kk = _dot_nt(kb16, kb16)
            qk = _dot_nt(qb16, kb16)
            for r in range(rep):
                beta_row = beta_refs[d][r, ci:ci + 1, :]
                g_row = ld_refs[d][r, ci:ci + 1, :]
                g_col = to_col(g_row)
                beta_col = to_col(beta_row)
                gc_col = jnp.sum(jnp.where(incl[d], g_row, 0.0), axis=1, keepdims=True)
                gc_row = jnp.sum(jnp.where(incl[1 - d], g_col, 0.0), axis=0, keepdims=True)
                dec = jnp.exp(jnp.minimum(gc_col - gc_row, 0.0))
                gc_end = gc_col[end[d]:end[d] + 1, :]
                eg = jnp.exp(gc_col)
                lower = jnp.where(strict[d], beta_col * kk * dec, 0.0)
                tinv = _unit_lower_inverse(lower, eye).astype(BF16)
                v = v_refs[d][rows, r * HEAD_DIM:(r + 1) * HEAD_DIM]
                st = st_ref[d, r]
                st16 = st.astype(BF16)
                sol_v = jnp.dot(tinv, (beta_col * v).astype(BF16), preferred_element_type=F32)
                sol_k = jnp.dot(tinv, (beta_col * eg * k).astype(BF16), preferred_element_type=F32)
                w = sol_v - jnp.dot(sol_k.astype(BF16), st16, preferred_element_type=F32)
                w16 = w.astype(BF16)
                a = jnp.where(incl[d], qk * dec, 0.0).astype(BF16)
                o = (eg * jnp.dot(qb16, st16, preferred_element_type=F32)
                     + jnp.dot(a, w16, preferred_element_type=F32))
                o_refs[d][rows, r * HEAD_DIM:(r + 1) * HEAD_DIM] = o
                kd_t = (k * jnp.exp(gc_end - gc_col)).T.astype(BF16)
                st_ref[d, r] = st * jnp.exp(gc_end) + jnp.dot(kd_t, w16, preferred_element_type=F32)


def _dn_scan_call(qkv, beta, ld, nb, t, gt, key_w, rep):
    ng = t // gt
    hd = HEAD_DIM
    hq = key_w // hd
    nch = gt // CHUNK
    koff = key_w // hd
    voff = 2 * key_w // (rep * hd)

    def spec(width, off, bwd):
        if bwd:
            return pl.BlockSpec((None, gt, width), lambda b, h, s: (b, _bwd_group(s, ng), off + h))
        return pl.BlockSpec((None, gt, width), lambda b, h, s: (b, s, off + h))

    def sspec(d):
        if d == 1:
            return pl.BlockSpec((None, None, rep, None, nch, CHUNK),
                                lambda b, h, s: (1, b, h, _bwd_group(s, ng), 0, 0))
        return pl.BlockSpec((None, None, rep, None, nch, CHUNK), lambda b, h, s: (0, b, h, s, 0, 0))

    in_specs, args = [], []
    for d in range(2):
        in_specs += [spec(hd, 0, d == 1), spec(hd, koff, d == 1), spec(rep * hd, voff, d == 1)]
        args += [qkv, qkv, qkv]
    in_specs += [sspec(0), sspec(0), sspec(1), sspec(1)]
    args += [beta, ld, beta, ld]
    out = jax.ShapeDtypeStruct((nb, t, hq * rep * hd), F32)
    kern = functools.partial(_dn_scan_kernel, gt=gt, rep=rep)
    return pl.pallas_call(
        kern,
        out_shape=(out, out),
        grid=(nb, hq, ng),
        in_specs=in_specs,
        out_specs=(pl.BlockSpec((None, gt, rep * hd), lambda b, h, s: (b, s, h)),
                   pl.BlockSpec((None, gt, rep * hd), lambda b, h, s: (b, _bwd_group(s, ng), h))),
        scratch_shapes=[pltpu.VMEM((2, rep, hd, hd), F32)],
        compiler_params=_params(("parallel", "parallel", "arbitrary")),
        name="deltanet_scan",
    )(*args)


def _final_norm_kernel(h_ref, w_ref, o_ref):
    x = h_ref[...]
    o_ref[...] = x * lax.rsqrt(jnp.mean(x * x, axis=-1, keepdims=True) + EPS) * w_ref[...]


def _final_norm_call(h3, w, ctx, seq):
    nb, t, d = h3.shape
    skip = ctx // ctx
    return pl.pallas_call(
        _final_norm_kernel,
        out_shape=jax.ShapeDtypeStruct((nb, seq, d), F32),
        grid=(nb, seq // ctx),
        in_specs=[pl.BlockSpec((None, ctx, d), lambda b, i: (b, i + skip, 0)),
                  pl.BlockSpec((1, d), lambda b, i: (0, 0))],
        out_specs=pl.BlockSpec((None, ctx, d), lambda b, i: (b, i, 0)),
        compiler_params=_params(("parallel", "parallel")),
        name="final_norm",
    )(h3, w.reshape(1, d))


def _scan_scalars(x, nb, t, gt):
    hv = x.shape[-1]
    x = x.reshape(2, nb, t, hv).transpose(0, 1, 3, 2)
    return x.reshape(2, nb, hv, t // gt, gt // CHUNK, CHUNK)


def kernel(x, c, ctx, c_ctx, w_ada, b_ada, norm_mix, norm_ffn, hg_lb_logits, hg_w_in, hg_w_f, hg_o_norm, hg_w_o, gla_w_in, gla_w_g1, gla_w_g2, gla_b_g, gla_o_norm, gla_w_o, dn_w_in, dn_conv, dn_w_b, dn_w_a, dn_a_log, dn_dt_bias, dn_o_norm, dn_w_o, ffn_w_gate, ffn_w_up, ffn_w_down, final_norm):
    nb, seq, d = x.shape
    n_ctx = ctx.shape[1]
    depth = w_ada.shape[0]
    t = n_ctx + seq
    gt = n_ctx
    assert n_ctx % CHUNK == 0 and seq % gt == 0 and GRID_W == CHUNK and nb + 1 <= MOD_ROWS
    dims = (t, n_ctx, nb)
    m = nb * t

    p = jax.nn.softmax(hg_lb_logits.astype(F32), axis=0)
    lower_bounds = jnp.cumsum(p, axis=0) - p[0]

    cc = jnp.zeros((MOD_ROWS, d), F32).at[:nb].set(c).at[nb].set(c_ctx)
    mods = _ada_call(cc, w_ada, b_ada)

    h = jnp.concatenate([ctx, x], axis=1).reshape(m, d)

    hg_heads = d // HEAD_DIM
    hg_kw = hg_w_f.shape[-1]
    gla_kw = gla_w_g2.shape[-1]
    gla_dk = gla_kw // GLA_HEADS
    gla_dv = d // GLA_HEADS
    dn_key_w = d
    dn_hv = dn_w_b.shape[-1]
    dn_val_w = dn_hv * HEAD_DIM
    dn_qkv_w = 2 * dn_key_w + dn_val_w
    rep = dn_hv // (dn_key_w // HEAD_DIM)

    for i in range(depth):
        kind, j = i % 3, i // 3
        a = _normmod_call(h, norm_mix, mods, i, 0, 1, dims)
        if kind == 0:
            kern = functools.partial(_hg_in_kernel, q_cols=hg_kw, scale=HEAD_DIM ** -0.5)
            proj = _proj_call(kern, a, hg_w_in, j, t, "hgrn2_in_proj", hg_kw)
            kk, lf = _hg_f_call(a, hg_w_f, j, lower_bounds[i], t)
            p3 = proj.reshape(nb, t, -1)
            kk4 = kk.reshape(2, nb, t, hg_kw)
            o_f, o_b = _gla_scan_call((p3, 0), ((kk4, 0, 0), (kk4, 1, 0)), (p3, hg_kw),
                                      lf.reshape(2, nb, t, hg_kw), nb, t, gt, hg_heads, HEAD_DIM,
                                      d // hg_heads)
            y = _gated_norm_call(o_f.reshape(m, -1), o_b.reshape(m, -1), proj, hg_kw + d, hg_o_norm, j, t)
            h = _residual_call(y, hg_w_o, j, h, mods, i, 2, dims, "hgrn2_out_proj")
        elif kind == 1:
            kern = functools.partial(_gla_in_kernel, q_cols=gla_kw, scale=gla_dk ** -0.5)
            proj = _proj_call(kern, a, gla_w_in, j, t, "gla_in_proj", gla_kw)
            lg = _gla_gate_call(a, gla_w_g1, gla_w_g2, gla_b_g, j, t)
            p3 = proj.reshape(nb, t, -1)
            o_f, o_b = _gla_scan_call((p3, 0), ((p3, None, gla_kw), (p3, None, gla_kw)), (p3, 2 * gla_kw),
                                      lg.reshape(2, nb, t, gla_kw), nb, t, gt, GLA_HEADS, gla_dk, gla_dv)
            y = _gated_norm_call(o_f.reshape(m, -1), o_b.reshape(m, -1), proj, 2 * gla_kw + d,
                                 gla_o_norm, j, t)
            h = _residual_call(y, gla_w_o, j, h, mods, i, 2, dims, "gla_out_proj")
        else:
            proj = _proj_call(_plain_kernel, a, dn_w_in, j, t, "deltanet_in_proj")
            beta, ld = _dn_bg_call(a, dn_w_b, dn_w_a, dn_a_log, dn_dt_bias, j, t)
            qkv = _dn_conv_call(proj.reshape(nb, t, -1), dn_conv, j, nb, t, gt, dn_qkv_w, dn_key_w)
            o_f, o_b = _dn_scan_call(qkv, _scan_scalars(beta, nb, t, gt), _scan_scalars(ld, nb, t, gt),
                                     nb, t, gt, dn_key_w, rep)
            y = _gated_norm_call(o_f.reshape(m, -1), o_b.reshape(m, -1), proj, dn_qkv_w, dn_o_norm, j, t)
            h = _residual_call(y, dn_w_o, j, h, mods, i, 2, dims, "deltanet_out_proj")
        a = _normmod_call(h, norm_ffn, mods, i, 3, 4, dims)
        u = _swiglu_call(a, ffn_w_gate, ffn_w_up, i, t)
        h = _residual_call(u, ffn_w_down, i, h, mods, i, 5, dims, "ffn_down_proj")

    return _final_norm_call(h.reshape(nb, t, d), final_norm, n_ctx, seq)
```

```python
import functools

import jax
import jax.numpy as jnp
from jax import lax
from jax.experimental import pallas as pl
from jax.experimental.pallas import tpu as pltpu

F32 = jnp.float32
BF16 = jnp.bfloat16

EPS = 1e-6
GATE_CLIP = 30.0
CHUNK = 64
GRID_W = 64
CONV_K = 5
GLA_HEADS = 4
GLA_GATE_NORM = 16.0
HEAD_DIM = 128
DECAY_SAFE = 60.0

V7X_VMEM_LIMIT = 56 * 1024 * 1024
MOD_ROWS = 8


def _params(sem):
    return pltpu.CompilerParams(dimension_semantics=sem, vmem_limit_bytes=V7X_VMEM_LIMIT)


def _sigmoid(x):
    return 1.0 / (1.0 + jnp.exp(-x))


def _silu(x):
    return x * _sigmoid(x)


def _softplus(x):
    return jnp.maximum(x, 0.0) + jnp.log1p(jnp.exp(-jnp.abs(x)))


def _log_sigmoid(x):
    return -_softplus(-x)


def _row_tile(t, target):
    best = 8
    for d in range(8, min(t, target) + 1, 8):
        if t % d == 0:
            best = d
    return best


def _col_tile(n, target):
    for c in (target, 512, 256, 128):
        if c <= target and n % c == 0:
            return c
    return n


def _mod_rows(m_ref, i, tm, tiles_per_batch, ctx, nb):
    b = i // tiles_per_batch
    r0 = (i % tiles_per_batch) * tm
    lat = m_ref[pl.ds(b, 1), :]
    cx = m_ref[nb:nb + 1, :]
    rows = lax.broadcasted_iota(jnp.int32, (tm, 1), 0) + r0
    return jnp.where(rows < ctx, cx, lat)


def _ada_kernel(c_ref, w_ref, b_ref, o_ref):
    s = _silu(c_ref[...]).astype(BF16)
    o_ref[...] = jnp.dot(s, w_ref[...].astype(BF16), preferred_element_type=F32) + b_ref[...]


def _ada_call(cc, w_ada, b_ada):
    depth, d, n = w_ada.shape
    tn = _col_tile(n, 1024)
    return pl.pallas_call(
        _ada_kernel,
        out_shape=jax.ShapeDtypeStruct((depth, MOD_ROWS, n), F32),
        grid=(depth, n // tn),
        in_specs=[pl.BlockSpec((MOD_ROWS, d), lambda i, j: (0, 0)),
                  pl.BlockSpec((None, d, tn), lambda i, j: (i, 0, j)),
                  pl.BlockSpec((None, 1, tn), lambda i, j: (i, 0, j))],
        out_specs=pl.BlockSpec((None, MOD_ROWS, tn), lambda i, j: (i, 0, j)),
        compiler_params=_params(("parallel", "parallel")),
        name="ada_mods",
    )(cc, w_ada, b_ada.reshape(depth, 1, n))


def _normmod_kernel(h_ref, nw_ref, sh_ref, sc_ref, o_ref, *, tm, tpb, ctx, nb):
    i = pl.program_id(0)
    x = h_ref[...]
    y = x * lax.rsqrt(jnp.mean(x * x, axis=-1, keepdims=True) + EPS) * nw_ref[...]
    shift = _mod_rows(sh_ref, i, tm, tpb, ctx, nb)
    scale = _mod_rows(sc_ref, i, tm, tpb, ctx, nb)
    o_ref[...] = (y * (1.0 + scale) + shift).astype(BF16)


def _normmod_call(h, norm_w, mods, layer, g_shift, g_scale, dims):
    m, d = h.shape
    t, ctx, nb = dims
    tm = _row_tile(t, 544)
    kern = functools.partial(_normmod_kernel, tm=tm, tpb=t // tm, ctx=ctx, nb=nb)
    return pl.pallas_call(
        kern,
        out_shape=jax.ShapeDtypeStruct((m, d), BF16),
        grid=(m // tm,),
        in_specs=[pl.BlockSpec((tm, d), lambda i: (i, 0)),
                  pl.BlockSpec((None, 1, d), lambda i: (layer, 0, 0)),
                  pl.BlockSpec((None, MOD_ROWS, d), lambda i: (layer, 0, g_shift)),
                  pl.BlockSpec((None, MOD_ROWS, d), lambda i: (layer, 0, g_scale))],
        out_specs=pl.BlockSpec((tm, d), lambda i: (i, 0)),
        compiler_params=_params(("parallel",)),
        name="norm_modulate",
    )(h, norm_w.reshape(norm_w.shape[0], 1, d), mods, mods)


def _dot(x, w_ref):
    return jnp.dot(x, w_ref[...].astype(BF16), preferred_element_type=F32)


def _mm_tiles(t, k, n, n_w, out_bytes, tm_target=1088, tn_target=512):
    budget = 40 * 1024 * 1024
    tm = _row_tile(t, tm_target)
    while tm > 64:
        for tn in sorted({tn_target, 512, 256}, reverse=True):
            if tn > tn_target:
                continue
            need = 2 * tm * k * 2 + 2 * n_w * k * tn * 4 + 2 * tm * tn * out_bytes + n_w * tm * tn * 4
            if n % tn == 0 and need <= budget:
                return tm, tn
        tm = _row_tile(t, tm - 8)
    return tm, _col_tile(n, 128)


def _hg_in_kernel(x_ref, w_ref, o_ref, *, tn, q_cols, scale):
    acc = _dot(x_ref[...], w_ref)
    o_ref[...] = jnp.where(pl.program_id(1) * tn < q_cols, _silu(acc) * scale, acc)


def _gla_in_kernel(x_ref, w_ref, o_ref, *, tn, q_cols, scale):
    acc = _dot(x_ref[...], w_ref)
    o_ref[...] = jnp.where(pl.program_id(1) * tn < q_cols, acc * scale, acc)


def _plain_kernel(x_ref, w_ref, o_ref, *, tn):
    o_ref[...] = _dot(x_ref[...], w_ref)


def _proj_call(kern, a, w, layer_idx, t, name, q_cols=0):
    m, k = a.shape
    n = w.shape[-1]
    tm, tn = _mm_tiles(t, k, n, 1, 4)
    while q_cols % tn:
        tn //= 2
    return pl.pallas_call(
        functools.partial(kern, tn=tn),
        out_shape=jax.ShapeDtypeStruct((m, n), F32),
        grid=(m // tm, n // tn),
        in_specs=[pl.BlockSpec((tm, k), lambda i, j: (i, 0)),
                  pl.BlockSpec((None, k, tn), lambda i, j: (layer_idx, 0, j))],
        out_specs=pl.BlockSpec((tm, tn), lambda i, j: (i, j)),
        compiler_params=_params(("parallel", "arbitrary")),
        name=name,
    )(a, w)


def _hg_f_kernel(x_ref, w_ref, lb_ref, k_ref, lf_ref):
    z = jnp.clip(_dot(x_ref[...], w_ref), -GATE_CLIP, GATE_CLIP)
    lb = lb_ref[...]
    lf_ref[...] = _log_sigmoid(z) + jnp.log1p(lb * jnp.exp(-z))
    k_ref[...] = (1.0 - lb) * _sigmoid(-z)


def _hg_f_call(a, w_f, layer_idx, lb, t):
    m, k = a.shape
    n = w_f.shape[-1]
    tm, tn = _mm_tiles(t, k, n, 1, 8)
    nt = n // tn
    out = jax.ShapeDtypeStruct((2, m, n), F32)
    return pl.pallas_call(
        _hg_f_kernel,
        out_shape=(out, out),
        grid=(m // tm, 2 * nt),
        in_specs=[pl.BlockSpec((tm, k), lambda i, j: (i, 0)),
                  pl.BlockSpec((None, None, k, tn), lambda i, j: (layer_idx, j // nt, 0, j % nt)),
                  pl.BlockSpec((1, tn), lambda i, j: (0, j % nt))],
        out_specs=(pl.BlockSpec((None, tm, tn), lambda i, j: (j // nt, i, j % nt)),
                   pl.BlockSpec((None, tm, tn), lambda i, j: (j // nt, i, j % nt))),
        compiler_params=_params(("parallel", "arbitrary")),
        name="hgrn2_gates",
    )(a, w_f, lb.reshape(1, n))


def _gla_gate_kernel(x_ref, w1_ref, w2_ref, b_ref, o_ref):
    low = _dot(x_ref[...], w1_ref)
    gl = _dot(low.astype(BF16), w2_ref) + b_ref[...]
    o_ref[...] = _log_sigmoid(gl) * (1.0 / GLA_GATE_NORM)


def _gla_gate_call(a, w_g1, w_g2, b_g, layer_idx, t):
    m, k = a.shape
    rank, kw = w_g2.shape[-2:]
    tm = _row_tile(t, 544)
    return pl.pallas_call(
        _gla_gate_kernel,
        out_shape=jax.ShapeDtypeStruct((2, m, kw), F32),
        grid=(m // tm, 2),
        in_specs=[pl.BlockSpec((tm, k), lambda i, d: (i, 0)),
                  pl.BlockSpec((None, None, k, rank), lambda i, d: (layer_idx, d, 0, 0)),
                  pl.BlockSpec((None, None, rank, kw), lambda i, d: (layer_idx, d, 0, 0)),
                  pl.BlockSpec((None, None, 1, kw), lambda i, d: (layer_idx, d, 0, 0))],
        out_specs=pl.BlockSpec((None, tm, kw), lambda i, d: (d, i, 0)),
        compiler_params=_params(("parallel", "arbitrary")),
        name="gla_gates",
    )(a, w_g1, w_g2, b_g.reshape(b_g.shape[0], 2, 1, kw))


def _dn_bg_kernel(x_ref, wb_ref, wa_ref, alog_ref, dtb_ref, beta_ref, ld_ref):
    x = x_ref[...]
    beta_ref[...] = _sigmoid(_dot(x, wb_ref))
    dt = _softplus(_dot(x, wa_ref) + dtb_ref[...])
    ld_ref[...] = -jnp.exp(alog_ref[...]) * dt


def _dn_bg_call(a, w_b, w_a, a_log, dt_bias, layer_idx, t):
    m, k = a.shape
    hv = w_b.shape[-1]
    tm = _row_tile(t, 1088)
    out = jax.ShapeDtypeStruct((2, m, hv), F32)
    wspec = pl.BlockSpec((None, None, k, hv), lambda i, d: (layer_idx, d, 0, 0))
    pspec = pl.BlockSpec((None, None, 1, hv), lambda i, d: (layer_idx, d, 0, 0))
    ospec = pl.BlockSpec((None, tm, hv), lambda i, d: (d, i, 0))
    nl = a_log.shape[0]
    return pl.pallas_call(
        _dn_bg_kernel,
        out_shape=(out, out),
        grid=(m // tm, 2),
        in_specs=[pl.BlockSpec((tm, k), lambda i, d: (i, 0)), wspec, wspec, pspec, pspec],
        out_specs=(ospec, ospec),
        compiler_params=_params(("parallel", "arbitrary")),
        name="deltanet_beta_decay",
    )(a, w_b, w_a, a_log.reshape(nl, 2, 1, hv), dt_bias.reshape(nl, 2, 1, hv))


def _swiglu_kernel(x_ref, wg_ref, wu_ref, o_ref):
    x = x_ref[...]
    o_ref[...] = (_silu(_dot(x, wg_ref)) * _dot(x, wu_ref)).astype(BF16)


def _swiglu_call(a, w_gate, w_up, layer, t):
    m, k = a.shape
    n = w_gate.shape[-1]
    tm, tn = _mm_tiles(t, k, n, 2, 2)
    wspec = pl.BlockSpec((None, k, tn), lambda i, j: (layer, 0, j))
    return pl.pallas_call(
        _swiglu_kernel,
        out_shape=jax.ShapeDtypeStruct((m, n), BF16),
        grid=(m // tm, n // tn),
        in_specs=[pl.BlockSpec((tm, k), lambda i, j: (i, 0)), wspec, wspec],
        out_specs=pl.BlockSpec((tm, tn), lambda i, j: (i, j)),
        compiler_params=_params(("parallel", "arbitrary")),
        name="ffn_swiglu",
    )(a, w_gate, w_up)


def _residual_kernel(x_ref, w_ref, h_ref, g_ref, o_ref, acc_ref, *, tm, tpb, ctx, nb, nk):
    kk = pl.program_id(2)
    part = _dot(x_ref[...], w_ref)

    @pl.when(kk == 0)
    def _():
        acc_ref[...] = part

    @pl.when(kk > 0)
    def _():
        acc_ref[...] += part

    @pl.when(kk == nk - 1)
    def _():
        gate = _mod_rows(g_ref, pl.program_id(0), tm, tpb, ctx, nb)
        o_ref[...] = h_ref[...] + gate * acc_ref[...]


K_SPLIT = 2048


def _k_tile(k):
    if k <= K_SPLIT:
        return k
    return max(tk for tk in range(128, K_SPLIT + 1, 128) if k % tk == 0)


def _residual_call(y, w, w_idx, h, mods, layer, g_gate, dims, name):
    m, k = y.shape
    d = h.shape[1]
    t, ctx, nb = dims
    tk = _k_tile(k)
    nk = k // tk
    tm, tn = _mm_tiles(t, tk, d, 1, 8, tn_target=1024 if nk > 1 else 512)
    nt = d // tn
    kern = functools.partial(_residual_kernel, tm=tm, tpb=t // tm, ctx=ctx, nb=nb, nk=nk)
    return pl.pallas_call(
        kern,
        out_shape=jax.ShapeDtypeStruct(h.shape, F32),
        grid=(m // tm, nt, nk),
        in_specs=[pl.BlockSpec((tm, tk), lambda i, j, kk: (i, kk)),
                  pl.BlockSpec((None, tk, tn), lambda i, j, kk: (w_idx, kk, j)),
                  pl.BlockSpec((tm, tn), lambda i, j, kk: (i, j)),
                  pl.BlockSpec((None, MOD_ROWS, tn), lambda i, j, kk: (layer, 0, g_gate * nt + j))],
        out_specs=pl.BlockSpec((tm, tn), lambda i, j, kk: (i, j)),
        scratch_shapes=[pltpu.VMEM((tm, tn), F32)],
        input_output_aliases={2: 0},
        compiler_params=_params(("parallel", "arbitrary", "arbitrary")),
        name=name,
    )(y, w, h, mods)


def _gated_norm_kernel(of_ref, ob_ref, g_ref, nw_ref, y_ref, *, dv, heads):
    nw = nw_ref[...]
    for hh in range(heads):
        cols = slice(hh * dv, (hh + 1) * dv)
        o = of_ref[:, cols] + ob_ref[:, cols]
        y = o * lax.rsqrt(jnp.mean(o * o, axis=-1, keepdims=True) + EPS) * nw
        y_ref[:, cols] = (y * _silu(g_ref[:, cols])).astype(BF16)


def _gated_norm_call(o_f, o_b, gate_arr, gate_col0, o_norm, norm_idx, t):
    m, width = o_f.shape
    dv = o_norm.shape[-1]
    tc = min(width, max(dv, 512))
    tm = _row_tile(t, 544)
    goff = gate_col0 // tc
    kern = functools.partial(_gated_norm_kernel, dv=dv, heads=tc // dv)
    return pl.pallas_call(
        kern,
        out_shape=jax.ShapeDtypeStruct((m, width), BF16),
        grid=(m // tm, width // tc),
        in_specs=[pl.BlockSpec((tm, tc), lambda i, j: (i, j)),
                  pl.BlockSpec((tm, tc), lambda i, j: (i, j)),
                  pl.BlockSpec((tm, tc), lambda i, j: (i, goff + j)),
                  pl.BlockSpec((None, 1, dv), lambda i, j: (norm_idx, 0, 0))],
        out_specs=pl.BlockSpec((tm, tc), lambda i, j: (i, j)),
        compiler_params=_params(("parallel", "parallel")),
        name="gated_head_norm",
    )(o_f, o_b, gate_arr, o_norm.reshape(o_norm.shape[0], 1, dv))


def _split3(x):
    x1 = x.astype(BF16)
    r1 = x - x1.astype(F32)
    x2 = r1.astype(BF16)
    x3 = (r1 - x2.astype(F32)).astype(BF16)
    return x1, x2, x3


def _chunk_cumsum(g, c, reverse):
    n = g.shape[0]
    pos = jnp.bitwise_and(lax.broadcasted_iota(jnp.int32, (n, 1), 0), c - 1)
    s = 1
    while s < c:
        if reverse:
            g = g + jnp.where(pos < c - s, pltpu.roll(g, n - s, 0), 0.0)
        else:
            g = g + jnp.where(pos >= s, pltpu.roll(g, s, 0), 0.0)
        s *= 2
    return g


def _bwd_group(s, ng):
    return jnp.where(s == 0, 0, ng - s)


def _dot_nt(a, b):
    return lax.dot_general(a, b, (((1,), (1,)), ((), ())), preferred_element_type=F32)


def _gla_scan_kernel(qf_ref, kf_ref, vf_ref, gf_ref, qb_ref, kb_ref, vb_ref, gb_ref,
                     of_ref, ob_ref, st_ref, b_ref, *, gt, dk, dv):
    c = CHUNK
    nch = gt // c

    @pl.when(pl.program_id(2) == 0)
    def _():
        st_ref[...] = jnp.zeros_like(st_ref)

    crow = lax.broadcasted_iota(jnp.int32, (c, c), 0)
    ccol = lax.broadcasted_iota(jnp.int32, (c, c), 1)
    masks = (ccol <= crow, ccol >= crow)
    mid = (c // 2 - 1, c // 2)
    end = (c - 1, 0)
    g_refs = (gf_ref, gb_ref)
    q_refs = (qf_ref, qb_ref)
    k_refs = (kf_ref, kb_ref)
    v_refs = (vf_ref, vb_ref)
    o_refs = (of_ref, ob_ref)

    worst = jnp.zeros((1, dk), F32)
    for d in range(2):
        b = _chunk_cumsum(g_refs[d][...], c, reverse=d == 1)
        b_ref[d] = b
        for ci in range(nch):
            r = b[ci * c + mid[d]:ci * c + mid[d] + 1, :]
            e = b[ci * c + end[d]:ci * c + end[d] + 1, :]
            worst = jnp.maximum(worst, jnp.maximum(-r, r - e))
    safe = jnp.max(worst) <= DECAY_SAFE

    def scores_fast(d, ci, q, k, b):
        r = b[mid[d]:mid[d] + 1, :]
        qs = (q * jnp.exp(b - r)).astype(BF16)
        ks = (k * jnp.exp(r - b)).astype(BF16)
        return jnp.where(masks[d], _dot_nt(qs, ks), 0.0)

    def scores_exact(d, ci, q, k, b):
        lane = lax.broadcasted_iota(jnp.int32, (c, c), 1)

        def body(s, acc):
            ks = k_refs[d][pl.ds(ci * c + s, 1), :]
            bs = b_ref[d, pl.ds(ci * c + s, 1), :]
            term = q * ks * jnp.exp(jnp.minimum(b - bs, 0.0))
            return jnp.where(lane == s, jnp.sum(term, axis=1, keepdims=True), acc)

        acc = lax.fori_loop(0, c, body, jnp.zeros((c, c), F32))
        return jnp.where(masks[d], acc, 0.0)

    def run(scores):
        pre = {}
        for d in range(2):
            for ci in range(nch):
                rows = slice(ci * c, (ci + 1) * c)
                q = q_refs[d][rows, :]
                k = k_refs[d][rows, :]
                v = v_refs[d][rows, :]
                b = b_ref[d, rows, :]
                b_end = b[end[d]:end[d] + 1, :]
                a = scores(d, ci, q, k, b).astype(BF16)
                ke = (k * jnp.exp(b_end - b)).astype(BF16)
                pre[d, ci] = dict(
                    av=jnp.dot(a, v.astype(BF16), preferred_element_type=F32),
                    qe=(q * jnp.exp(b)).astype(BF16),
                    upd=jnp.dot(v.T.astype(BF16), ke, preferred_element_type=F32),
                    dec=jnp.exp(b_end))
        for d in range(2):
            st = st_ref[d]
            for step in range(nch):
                ci = step if d == 0 else nch - 1 - step
                p = pre[d, ci]
                o_refs[d][ci * c:(ci + 1) * c, :] = p["av"] + _dot_nt(p["qe"], st.astype(BF16))
                st = st * p["dec"] + p["upd"]
            st_ref[d] = st

    @pl.when(safe)
    def _():
        run(scores_fast)

    @pl.when(jnp.logical_not(safe))
    def _():
        run(scores_exact)


def _gla_scan_call(q_src, k_srcs, v_src, g_arr, nb, t, gt, heads, dk, dv):
    ng = t // gt

    def spec3(col0, width, bwd):
        off = col0 // width
        if bwd:
            return pl.BlockSpec((None, gt, width), lambda b, h, s: (b, _bwd_group(s, ng), off + h))
        return pl.BlockSpec((None, gt, width), lambda b, h, s: (b, s, off + h))

    def spec4(lead, col0, width, bwd):
        off = col0 // width
        if bwd:
            return pl.BlockSpec((None, None, gt, width),
                                lambda b, h, s: (lead, b, _bwd_group(s, ng), off + h))
        return pl.BlockSpec((None, None, gt, width), lambda b, h, s: (lead, b, s, off + h))

    def kspec(src, bwd):
        arr, lead, col0 = src
        return spec3(col0, dk, bwd) if lead is None else spec4(lead, col0, dk, bwd)

    in_specs, args = [], []
    for d in range(2):
        bwd = d == 1
        in_specs += [spec3(q_src[1], dk, bwd), kspec(k_srcs[d], bwd), spec3(v_src[1], dv, bwd),
                     spec4(d, 0, dk, bwd)]
        args += [q_src[0], k_srcs[d][0], v_src[0], g_arr]
    out = jax.ShapeDtypeStruct((nb, t, heads * dv), F32)
    kern = functools.partial(_gla_scan_kernel, gt=gt, dk=dk, dv=dv)
    return pl.pallas_call(
        kern,
        out_shape=(out, out),
        grid=(nb, heads, ng),
        in_specs=in_specs,
        out_specs=(pl.BlockSpec((None, gt, dv), lambda b, h, s: (b, s, h)),
                   pl.BlockSpec((None, gt, dv), lambda b, h, s: (b, _bwd_group(s, ng), h))),
        scratch_shapes=[pltpu.VMEM((2, dv, dk), F32), pltpu.VMEM((2, gt, dk), F32)],
        compiler_params=_params(("parallel", "parallel", "arbitrary")),
        name="gla_scan",
    )(*args)


def _dn_conv_kernel(x_ref, w_ref, o_ref, *, gt, n_norm_tiles, n_q_tiles, qscale):
    s = pl.program_id(1)
    j = pl.program_id(2)
    x = x_ref[...]
    w = w_ref[...]
    seg = jnp.where(s == 0, gt, GRID_W)
    r = lax.broadcasted_iota(jnp.int32, (gt, 1), 0)
    pos = jnp.where(s == 0, r, jnp.bitwise_and(r, GRID_W - 1))
    acc = x * w[CONV_K // 2:CONV_K // 2 + 1, :]
    for tap in range(CONV_K):
        off = tap - CONV_K // 2
        if off == 0:
            continue
        shifted = pltpu.roll(x, (-off) % gt, 0)
        ok = (pos + off >= 0) & (pos + off < seg)
        acc = acc + jnp.where(ok, shifted, 0.0) * w[tap:tap + 1, :]
    y = _silu(acc)
    o_ref[...] = y

    @pl.when(j < n_norm_tiles)
    def _():
        scale = jnp.where(j < n_q_tiles, qscale, 1.0)
        for hh in range(y.shape[1] // HEAD_DIM):
            cols = slice(hh * HEAD_DIM, (hh + 1) * HEAD_DIM)
            yh = y[:, cols]
            o_ref[:, cols] = yh * (lax.rsqrt(jnp.sum(yh * yh, axis=-1, keepdims=True) + EPS) * scale)


def _dn_conv_call(p, conv_w, layer_idx, nb, t, gt, qkv_w, key_w):
    tc = min(512, key_w)
    kern = functools.partial(_dn_conv_kernel, gt=gt, n_norm_tiles=2 * key_w // tc,
                             n_q_tiles=key_w // tc, qscale=HEAD_DIM ** -0.5)
    return pl.pallas_call(
        kern,
        out_shape=jax.ShapeDtypeStruct((nb, t, qkv_w), F32),
        grid=(nb, t // gt, qkv_w // tc),
        in_specs=[pl.BlockSpec((None, gt, tc), lambda b, s, j: (b, s, j)),
                  pl.BlockSpec((None, CONV_K, tc), lambda b, s, j: (layer_idx, 0, j))],
        out_specs=pl.BlockSpec((None, gt, tc), lambda b, s, j: (b, s, j)),
        compiler_params=_params(("parallel", "parallel", "parallel")),
        name="deltanet_conv",
    )(p, conv_w)


def _mm3(a, b):
    a1 = a.astype(BF16)
    a2 = (a - a1.astype(F32)).astype(BF16)
    b1 = b.astype(BF16)
    b2 = (b - b1.astype(F32)).astype(BF16)
    return (jnp.dot(a1, b1, preferred_element_type=F32) + jnp.dot(a1, b2, preferred_element_type=F32)
            + jnp.dot(a2, b1, preferred_element_type=F32))


def _mm1(a, b):
    return jnp.dot(a.astype(BF16), b.astype(BF16), preferred_element_type=F32)


INVERSE_EXACT_LEVELS = 1


def _neumann_inverses(ms, eye):
    c = CHUNK
    ps = [eye + m for m in ms]
    xs = [_mm3(m, m) for m in ms]
    levels = c.bit_length() - 2
    for lvl in range(levels):
        mm = _mm3 if lvl < INVERSE_EXACT_LEVELS else _mm1
        if lvl == levels - 1:
            ps = [p + mm(p, x) for p, x in zip(ps, xs)]
        else:
            rs = [mm(jnp.concatenate([x, p], axis=0), x) for p, x in zip(ps, xs)]
            xs = [r[:c] for r in rs]
            ps = [p + r[c:] for p, r in zip(ps, rs)]
    return ps


def _dn_scan_kernel(qf_ref, kf_ref, vf_ref, qb_ref, kb_ref, vb_ref, beta_f_ref, ld_f_ref,
                    beta_b_ref, ld_b_ref, of_ref, ob_ref, st_ref, *, gt, rep):
    c = CHUNK
    nch = gt // c

    @pl.when(pl.program_id(2) == 0)
    def _():
        st_ref[...] = jnp.zeros_like(st_ref)

    row = lax.broadcasted_iota(jnp.int32, (c, c), 0)
    col = lax.broadcasted_iota(jnp.int32, (c, c), 1)
    diag = row == col
    eye = jnp.where(diag, 1.0, 0.0)
    incl = (col <= row, col >= row)
    strict = (col < row, col > row)
    end = (c - 1, 0)
    q_refs = (qf_ref, qb_ref)
    k_refs = (kf_ref, kb_ref)
    v_refs = (vf_ref, vb_ref)
    o_refs = (of_ref, ob_ref)
    beta_refs = (beta_f_ref, beta_b_ref)
    ld_refs = (ld_f_ref, ld_b_ref)

    def to_col(x_row):
        return jnp.sum(jnp.where(diag, x_row, 0.0), axis=1, keepdims=True)

    units = {}
    for d in range(2):
        for ci in range(nch):
            rows = slice(ci * c, (ci + 1) * c)
            q = q_refs[d][rows, :]
            k = k_refs[d][rows, :]
            qb16 = q.astype(BF16)
            kb16 = k.astype(BF16)
            kk = _dot_nt(kb16, kb16)
            qk = _dot_nt(qb16, kb16)
            for r in range(rep):
                beta_row = beta_refs[d][r, ci:ci + 1, :]
                g_row = ld_refs[d][r, ci:ci + 1, :]
                g_col = to_col(g_row)
                beta_col = to_col(beta_row)
                gc_col = jnp.sum(jnp.where(incl[d], g_row, 0.0), axis=1, keepdims=True)
                gc_row = jnp.sum(jnp.where(incl[1 - d], g_col, 0.0), axis=0, keepdims=True)
                dec = jnp.exp(jnp.minimum(gc_col - gc_row, 0.0))
                gc_end = gc_col[end[d]:end[d] + 1, :]
                eg = jnp.exp(gc_col)
                v = v_refs[d][rows, r * HEAD_DIM:(r + 1) * HEAD_DIM]
                units[d, ci, r] = dict(
                    m=jnp.where(strict[d], -(beta_col * kk * dec), 0.0),
                    rhs_v=(beta_col * v).astype(BF16),
                    rhs_k=(beta_col * eg * k).astype(BF16),
                    a=jnp.where(incl[d], qk * dec, 0.0).astype(BF16),
                    kd_t=(k * jnp.exp(gc_end - gc_col)).T.astype(BF16),
                    eg=eg, dec_end=jnp.exp(gc_end), qb16=qb16)

    keys = list(units)
    for key, tinv in zip(keys, _neumann_inverses([units[key]["m"] for key in keys], eye)):
        u = units[key]
        sol = jnp.dot(tinv.astype(BF16), jnp.concatenate([u["rhs_v"], u["rhs_k"]], axis=1),
                      preferred_element_type=F32)
        u["sol_v"] = sol[:, :HEAD_DIM]
        u["kq"] = jnp.concatenate([sol[:, HEAD_DIM:].astype(BF16), u["qb16"]], axis=0)
        u["a_kd"] = jnp.concatenate([u["a"], u["kd_t"]], axis=0)

    chains = [(d, r) for d in range(2) for r in range(rep)]
    states = {ch: st_ref[ch[0], ch[1]] for ch in chains}
    for step in range(nch):
        for d, r in chains:
            ci = step if d == 0 else nch - 1 - step
            u = units[d, ci, r]
            st = states[d, r]
            ks_qs = jnp.dot(u["kq"], st.astype(BF16), preferred_element_type=F32)
            w16 = (u["sol_v"] - ks_qs[:c]).astype(BF16)
            aw_kw = jnp.dot(u["a_kd"], w16, preferred_element_type=F32)
            o_refs[d][ci * c:(ci + 1) * c, r * HEAD_DIM:(r + 1) * HEAD_DIM] = u["eg"] * ks_qs[c:] + aw_kw[:c]
            states[d, r] = st * u["dec_end"] + aw_kw[c:]
    for d, r in chains:
        st_ref[d, r] = states[d, r]


def _dn_scan_call(qkv, beta, ld, nb, t, gt, key_w, rep):
    ng = t // gt
    hd = HEAD_DIM
    hq = key_w // hd
    nch = gt // CHUNK
    koff = key_w // hd
    voff = 2 * key_w // (rep * hd)

    def spec(width, off, bwd):
        if bwd:
            return pl.BlockSpec((None, gt, width), lambda b, h, s: (b, _bwd_group(s, ng), off + h))
        return pl.BlockSpec((None, gt, width), lambda b, h, s: (b, s, off + h))

    def sspec(d):
        if d == 1:
            return pl.BlockSpec((None, None, rep, None, nch, CHUNK),
                                lambda b, h, s: (1, b, h, _bwd_group(s, ng), 0, 0))
        return pl.BlockSpec((None, None, rep, None, nch, CHUNK), lambda b, h, s: (0, b, h, s, 0, 0))

    in_specs, args = [], []
    for d in range(2):
        in_specs += [spec(hd, 0, d == 1), spec(hd, koff, d == 1), spec(rep * hd, voff, d == 1)]
        args += [qkv, qkv, qkv]
    in_specs += [sspec(0), sspec(0), sspec(1), sspec(1)]
    args += [beta, ld, beta, ld]
    out = jax.ShapeDtypeStruct((nb, t, hq * rep * hd), F32)
    kern = functools.partial(_dn_scan_kernel, gt=gt, rep=rep)
    return pl.pallas_call(
        kern,
        out_shape=(out, out),
        grid=(nb, hq, ng),
        in_specs=in_specs,
        out_specs=(pl.BlockSpec((None, gt, rep * hd), lambda b, h, s: (b, s, h)),
                   pl.BlockSpec((None, gt, rep * hd), lambda b, h, s: (b, _bwd_group(s, ng), h))),
        scratch_shapes=[pltpu.VMEM((2, rep, hd, hd), F32)],
        compiler_params=_params(("parallel", "parallel", "arbitrary")),
        name="deltanet_scan",
    )(*args)


def _final_norm_kernel(h_ref, w_ref, o_ref):
    x = h_ref[...]
    o_ref[...] = x * lax.rsqrt(jnp.mean(x * x, axis=-1, keepdims=True) + EPS) * w_ref[...]


def _final_norm_call(h3, w, ctx, seq):
    nb, t, d = h3.shape
    skip = ctx // ctx
    return pl.pallas_call(
        _final_norm_kernel,
        out_shape=jax.ShapeDtypeStruct((nb, seq, d), F32),
        grid=(nb, seq // ctx),
        in_specs=[pl.BlockSpec((None, ctx, d), lambda b, i: (b, i + skip, 0)),
                  pl.BlockSpec((1, d), lambda b, i: (0, 0))],
        out_specs=pl.BlockSpec((None, ctx, d), lambda b, i: (b, i, 0)),
        compiler_params=_params(("parallel", "parallel")),
        name="final_norm",
    )(h3, w.reshape(1, d))


def _scan_scalars(x, nb, t, gt):
    hv = x.shape[-1]
    x = x.reshape(2, nb, t, hv).transpose(0, 1, 3, 2)
    return x.reshape(2, nb, hv, t // gt, gt // CHUNK, CHUNK)


def kernel(x, c, ctx, c_ctx, w_ada, b_ada, norm_mix, norm_ffn, hg_lb_logits, hg_w_in, hg_w_f, hg_o_norm, hg_w_o, gla_w_in, gla_w_g1, gla_w_g2, gla_b_g, gla_o_norm, gla_w_o, dn_w_in, dn_conv, dn_w_b, dn_w_a, dn_a_log, dn_dt_bias, dn_o_norm, dn_w_o, ffn_w_gate, ffn_w_up, ffn_w_down, final_norm):
    nb, seq, d = x.shape
    n_ctx = ctx.shape[1]
    depth = w_ada.shape[0]
    t = n_ctx + seq
    gt = n_ctx
    assert n_ctx % CHUNK == 0 and seq % gt == 0 and GRID_W == CHUNK and nb + 1 <= MOD_ROWS
    dims = (t, n_ctx, nb)
    m = nb * t

    p = jax.nn.softmax(hg_lb_logits.astype(F32), axis=0)
    lower_bounds = jnp.cumsum(p, axis=0) - p[0]

    cc = jnp.zeros((MOD_ROWS, d), F32).at[:nb].set(c).at[nb].set(c_ctx)
    mods = _ada_call(cc, w_ada, b_ada)

    h = jnp.concatenate([ctx, x], axis=1).reshape(m, d)

    hg_heads = d // HEAD_DIM
    hg_kw = hg_w_f.shape[-1]
    gla_kw = gla_w_g2.shape[-1]
    gla_dk = gla_kw // GLA_HEADS
    gla_dv = d // GLA_HEADS
    dn_key_w = d
    dn_hv = dn_w_b.shape[-1]
    dn_val_w = dn_hv * HEAD_DIM
    dn_qkv_w = 2 * dn_key_w + dn_val_w
    rep = dn_hv // (dn_key_w // HEAD_DIM)

    for i in range(depth):
        kind, j = i % 3, i // 3
        a = _normmod_call(h, norm_mix, mods, i, 0, 1, dims)
        if kind == 0:
            kern = functools.partial(_hg_in_kernel, q_cols=hg_kw, scale=HEAD_DIM ** -0.5)
            proj = _proj_call(kern, a, hg_w_in, j, t, "hgrn2_in_proj", hg_kw)
            kk, lf = _hg_f_call(a, hg_w_f, j, lower_bounds[i], t)
            p3 = proj.reshape(nb, t, -1)
            kk4 = kk.reshape(2, nb, t, hg_kw)
            o_f, o_b = _gla_scan_call((p3, 0), ((kk4, 0, 0), (kk4, 1, 0)), (p3, hg_kw),
                                      lf.reshape(2, nb, t, hg_kw), nb, t, gt, hg_heads, HEAD_DIM,
                                      d // hg_heads)
            y = _gated_norm_call(o_f.reshape(m, -1), o_b.reshape(m, -1), proj, hg_kw + d, hg_o_norm, j, t)
            h = _residual_call(y, hg_w_o, j, h, mods, i, 2, dims, "hgrn2_out_proj")
        elif kind == 1:
            kern = functools.partial(_gla_in_kernel, q_cols=gla_kw, scale=gla_dk ** -0.5)
            proj = _proj_call(kern, a, gla_w_in, j, t, "gla_in_proj", gla_kw)
            lg = _gla_gate_call(a, gla_w_g1, gla_w_g2, gla_b_g, j, t)
            p3 = proj.reshape(nb, t, -1)
            o_f, o_b = _gla_scan_call((p3, 0), ((p3, None, gla_kw), (p3, None, gla_kw)), (p3, 2 * gla_kw),
                                      lg.reshape(2, nb, t, gla_kw), nb, t, gt, GLA_HEADS, gla_dk, gla_dv)
            y = _gated_norm_call(o_f.reshape(m, -1), o_b.reshape(m, -1), proj, 2 * gla_kw + d,
                                 gla_o_norm, j, t)
            h = _residual_call(y, gla_w_o, j, h, mods, i, 2, dims, "gla_out_proj")
        else:
            proj = _proj_call(_plain_kernel, a, dn_w_in, j, t, "deltanet_in_proj")
            beta, ld = _dn_bg_call(a, dn_w_b, dn_w_a, dn_a_log, dn_dt_bias, j, t)
            qkv = _dn_conv_call(proj.reshape(nb, t, -1), dn_conv, j, nb, t, gt, dn_qkv_w, dn_key_w)
            o_f, o_b = _dn_scan_call(qkv, _scan_scalars(beta, nb, t, gt), _scan_scalars(ld, nb, t, gt),
                                     nb, t, gt, dn_key_w, rep)
            y = _gated_norm_call(o_f.reshape(m, -1), o_b.reshape(m, -1), proj, dn_qkv_w, dn_o_norm, j, t)
            h = _residual_call(y, dn_w_o, j, h, mods, i, 2, dims, "deltanet_out_proj")
        a = _normmod_call(h, norm_ffn, mods, i, 3, 4, dims)
        u = _swiglu_call(a, ffn_w_gate, ffn_w_up, i, t)
        h = _residual_call(u, ffn_w_down, i, h, mods, i, 5, dims, "ffn_down_proj")

    return _final_norm_call(h.reshape(nb, t, d), final_norm, n_ctx, seq)
```

```python
import functools
from typing import NamedTuple

import jax
import jax.numpy as jnp
from jax import lax
from jax.experimental import pallas as pl
from jax.experimental.pallas import tpu as pltpu

F32 = jnp.float32
BF16 = jnp.bfloat16

EPS = 1e-6
GATE_CLIP = 30.0
CHUNK = 64
GRID_W = 64
CONV_K = 5
GLA_HEADS = 4
GLA_GATE_NORM = 16.0
HEAD_DIM = 128
EXP_RANGE = 80.0

V7X_VMEM_LIMIT = 56 * 1024 * 1024
VMEM_BLOCK_BUDGET = 44 * 1024 * 1024
BF16_SUBLANES = 16
MOD_ROWS = 8


def _params(sem):
    return pltpu.CompilerParams(dimension_semantics=sem, vmem_limit_bytes=V7X_VMEM_LIMIT)


def _sigmoid(x):
    return 1.0 / (1.0 + jnp.exp(-x))


def _silu(x):
    return x * _sigmoid(x)


def _softplus(x):
    return jnp.maximum(x, 0.0) + jnp.log1p(jnp.exp(-jnp.abs(x)))


def _log_sigmoid(x):
    return -_softplus(-x)


def _row_tile(t, target):
    best = 8
    for d in range(8, min(t, target) + 1, 8):
        if t % d == 0:
            best = d
    return best


def _col_tile(n, target):
    for c in (target, 512, 256, 128):
        if c <= target and n % c == 0:
            return c
    return n


def _mod_rows(m_ref, i, tm, tiles_per_batch, ctx, nb):
    b = i // tiles_per_batch
    r0 = (i % tiles_per_batch) * tm
    lat = m_ref[pl.ds(b, 1), :]
    cx = m_ref[nb:nb + 1, :]
    rows = lax.broadcasted_iota(jnp.int32, (tm, 1), 0) + r0
    return jnp.where(rows < ctx, cx, lat)


def _ada_kernel(c_ref, w_ref, b_ref, o_ref):
    s = _silu(c_ref[...]).astype(BF16)
    o_ref[...] = jnp.dot(s, w_ref[...].astype(BF16), preferred_element_type=F32) + b_ref[...]


def _ada_call(cc, w_ada, b_ada):
    depth, d, n = w_ada.shape
    tn = _col_tile(n, 1024)
    return pl.pallas_call(
        _ada_kernel,
        out_shape=jax.ShapeDtypeStruct((depth, MOD_ROWS, n), F32),
        grid=(depth, n // tn),
        in_specs=[pl.BlockSpec((MOD_ROWS, d), lambda i, j: (0, 0)),
                  pl.BlockSpec((None, d, tn), lambda i, j: (i, 0, j)),
                  pl.BlockSpec((None, 1, tn), lambda i, j: (i, 0, j))],
        out_specs=pl.BlockSpec((None, MOD_ROWS, tn), lambda i, j: (i, 0, j)),
        compiler_params=_params(("parallel", "parallel")),
        name="ada_mods",
    )(cc, w_ada, b_ada.reshape(depth, 1, n))


class _NormSrc(NamedTuple):
    h: jax.Array
    norm_w: jax.Array
    mods: jax.Array
    layer: int
    g_shift: int
    g_scale: int
    dims: tuple


def _dot(x, w_ref):
    return jnp.dot(x, w_ref[...].astype(BF16), preferred_element_type=F32)


def _prologue_rows(tm):
    best = tm
    for rc in range(BF16_SUBLANES, min(tm, 272) + 1, BF16_SUBLANES):
        if tm % rc == 0:
            best = rc
    return best


def _fused(body, src, tm):
    t, ctx, nb = src.dims
    tpb = t // tm
    rc = _prologue_rows(tm)

    def kern(h_ref, nw_ref, sh_ref, sc_ref, *rest):
        a_ref = rest[-1]

        @pl.when(pl.program_id(1) == 0)
        def _():
            i = pl.program_id(0)
            b = i // tpb
            r0 = (i % tpb) * tm
            nw = nw_ref[...]

            def rows(r, carry):
                start = pl.multiple_of(r * rc, BF16_SUBLANES)
                x = h_ref[pl.ds(start, rc), :]
                y = x * lax.rsqrt(jnp.mean(x * x, axis=-1, keepdims=True) + EPS) * nw
                is_ctx = lax.broadcasted_iota(jnp.int32, (rc, 1), 0) + (r0 + start) < ctx
                scale = jnp.where(is_ctx, sc_ref[nb:nb + 1, :], sc_ref[pl.ds(b, 1), :])
                shift = jnp.where(is_ctx, sh_ref[nb:nb + 1, :], sh_ref[pl.ds(b, 1), :])
                a_ref[pl.ds(start, rc), :] = (y * (1.0 + scale) + shift).astype(BF16)
                return carry

            lax.fori_loop(0, tm // rc, rows, 0)

        body(a_ref, *rest[:-1])

    return kern


def _src_specs(src, tm):
    d = src.h.shape[1]
    specs = [pl.BlockSpec((tm, d), lambda i, j: (i, 0), pipeline_mode=pl.Buffered(1)),
             pl.BlockSpec((None, 1, d), lambda i, j: (src.layer, 0, 0)),
             pl.BlockSpec((None, MOD_ROWS, d), lambda i, j: (src.layer, 0, src.g_shift)),
             pl.BlockSpec((None, MOD_ROWS, d), lambda i, j: (src.layer, 0, src.g_scale))]
    return specs, [src.h, src.norm_w, src.mods, src.mods]


def _fused_call(body, src, tm, grid_inner, in_specs, args, out_shape, out_specs, name):
    m, d = src.h.shape
    specs, src_args = _src_specs(src, tm)
    return pl.pallas_call(
        _fused(body, src, tm),
        out_shape=out_shape,
        grid=(m // tm, grid_inner),
        in_specs=specs + in_specs,
        out_specs=out_specs,
        scratch_shapes=[pltpu.VMEM((tm, d), BF16)],
        compiler_params=_params(("parallel", "arbitrary")),
        name=name,
    )(*src_args, *args)


def _mm_tiles(t, k, n, n_w, out_bytes, x_bytes=4, fixed=0, tm_target=1088, tn_target=512):
    tm = _row_tile(t, tm_target)
    while tm > 64:
        for tn in sorted({tn_target, 512, 256}, reverse=True):
            if tn > tn_target:
                continue
            need = tm * k * x_bytes + fixed + 2 * n_w * k * tn * 4 + 2 * tm * tn * out_bytes + n_w * tm * tn * 4
            if n % tn == 0 and need <= VMEM_BLOCK_BUDGET:
                return tm, tn
        tm = _row_tile(t, tm - 8)
    return tm, _col_tile(n, 128)


def _fused_tiles(t, k, n, n_w, out_bytes):
    return _mm_tiles(t, k, n, n_w, out_bytes, x_bytes=6, fixed=3 * 272 * k * 4)


def _hg_in_body(x_ref, w_ref, o_ref, *, tn, q_cols, scale):
    acc = _dot(x_ref[...], w_ref)
    o_ref[...] = jnp.where(pl.program_id(1) * tn < q_cols, _silu(acc) * scale, acc)


def _gla_in_body(x_ref, w_ref, o_ref, *, tn, q_cols, scale):
    acc = _dot(x_ref[...], w_ref)
    o_ref[...] = jnp.where(pl.program_id(1) * tn < q_cols, acc * scale, acc)


def _plain_body(x_ref, w_ref, o_ref, *, tn):
    o_ref[...] = _dot(x_ref[...], w_ref)


def _proj_call(body, src, w, layer_idx, name, q_cols=0):
    m, k = src.h.shape
    n = w.shape[-1]
    tm, tn = _fused_tiles(src.dims[0], k, n, 1, 4)
    while q_cols % tn:
        tn //= 2
    return _fused_call(
        functools.partial(body, tn=tn), src, tm, n // tn,
        [pl.BlockSpec((None, k, tn), lambda i, j: (layer_idx, 0, j))], [w],
        jax.ShapeDtypeStruct((m, n), F32), pl.BlockSpec((tm, tn), lambda i, j: (i, j)), name)


def _hg_f_body(x_ref, w_ref, lb_ref, k_ref, lf_ref):
    z = jnp.clip(_dot(x_ref[...], w_ref), -GATE_CLIP, GATE_CLIP)
    lb = lb_ref[...]
    u = jnp.exp(-z)
    r = 1.0 / (1.0 + u)
    lf_ref[...] = jnp.log((1.0 + lb * u) * r)
    k_ref[...] = (1.0 - lb) * (u * r)


def _hg_f_call(src, w_f, layer_idx, lb):
    m, k = src.h.shape
    n = w_f.shape[-1]
    tm, tn = _fused_tiles(src.dims[0], k, n, 1, 8)
    nt = n // tn
    out = jax.ShapeDtypeStruct((2, m, n), F32)
    ospec = pl.BlockSpec((None, tm, tn), lambda i, j: (j // nt, i, j % nt))
    return _fused_call(
        _hg_f_body, src, tm, 2 * nt,
        [pl.BlockSpec((None, None, k, tn), lambda i, j: (layer_idx, j // nt, 0, j % nt)),
         pl.BlockSpec((1, tn), lambda i, j: (0, j % nt))],
        [w_f, lb.reshape(1, n)], (out, out), (ospec, ospec), "hgrn2_gates")


def _gla_gate_body(x_ref, w1_ref, w2_ref, b_ref, o_ref):
    low = _dot(x_ref[...], w1_ref)
    gl = _dot(low.astype(BF16), w2_ref) + b_ref[...]
    o_ref[...] = _log_sigmoid(gl) * (1.0 / GLA_GATE_NORM)


def _gla_gate_call(src, w_g1, w_g2, b_g, layer_idx):
    m, k = src.h.shape
    rank, kw = w_g2.shape[-2:]
    tm = _row_tile(src.dims[0], 544)
    return _fused_call(
        _gla_gate_body, src, tm, 2,
        [pl.BlockSpec((None, None, k, rank), lambda i, d: (layer_idx, d, 0, 0)),
         pl.BlockSpec((None, None, rank, kw), lambda i, d: (layer_idx, d, 0, 0)),
         pl.BlockSpec((None, None, 1, kw), lambda i, d: (layer_idx, d, 0, 0))],
        [w_g1, w_g2, b_g.reshape(b_g.shape[0], 2, 1, kw)],
        jax.ShapeDtypeStruct((2, m, kw), F32), pl.BlockSpec((None, tm, kw), lambda i, d: (d, i, 0)),
        "gla_gates")


def _dn_bg_body(x_ref, wb_ref, wa_ref, alog_ref, dtb_ref, beta_ref, ld_ref):
    x = x_ref[...]
    beta_ref[...] = _sigmoid(_dot(x, wb_ref))
    dt = _softplus(_dot(x, wa_ref) + dtb_ref[...])
    ld_ref[...] = -jnp.exp(alog_ref[...]) * dt


def _dn_bg_call(src, w_b, w_a, a_log, dt_bias, layer_idx):
    m, k = src.h.shape
    hv = w_b.shape[-1]
    tm = _row_tile(src.dims[0], 544)
    out = jax.ShapeDtypeStruct((2, m, hv), F32)
    wspec = pl.BlockSpec((None, None, k, hv), lambda i, d: (layer_idx, d, 0, 0))
    pspec = pl.BlockSpec((None, None, 1, hv), lambda i, d: (layer_idx, d, 0, 0))
    ospec = pl.BlockSpec((None, tm, hv), lambda i, d: (d, i, 0))
    nl = a_log.shape[0]
    return _fused_call(
        _dn_bg_body, src, tm, 2, [wspec, wspec, pspec, pspec],
        [w_b, w_a, a_log.reshape(nl, 2, 1, hv), dt_bias.reshape(nl, 2, 1, hv)],
        (out, out), (ospec, ospec), "deltanet_beta_decay")


def _swiglu_body(x_ref, wg_ref, wu_ref, o_ref):
    x = x_ref[...]
    o_ref[...] = (_silu(_dot(x, wg_ref)) * _dot(x, wu_ref)).astype(BF16)


def _swiglu_call(src, w_gate, w_up, layer):
    m, k = src.h.shape
    n = w_gate.shape[-1]
    tm, tn = _fused_tiles(src.dims[0], k, n, 2, 2)
    wspec = pl.BlockSpec((None, k, tn), lambda i, j: (layer, 0, j))
    return _fused_call(
        _swiglu_body, src, tm, n // tn, [wspec, wspec], [w_gate, w_up],
        jax.ShapeDtypeStruct((m, n), BF16), pl.BlockSpec((tm, tn), lambda i, j: (i, j)), "ffn_swiglu")


def _residual_kernel(x_ref, w_ref, h_ref, g_ref, o_ref, acc_ref, *, tm, tpb, ctx, nb, nk):
    kk = pl.program_id(2)
    part = _dot(x_ref[...], w_ref)

    @pl.when(kk == 0)
    def _():
        acc_ref[...] = part

    @pl.when(kk > 0)
    def _():
        acc_ref[...] += part

    @pl.when(kk == nk - 1)
    def _():
        gate = _mod_rows(g_ref, pl.program_id(0), tm, tpb, ctx, nb)
        o_ref[...] = h_ref[...] + gate * acc_ref[...]


K_SPLIT = 4096


def _k_tile(k):
    if k <= K_SPLIT:
        return k
    return max(tk for tk in range(128, K_SPLIT + 1, 128) if k % tk == 0)


def _residual_call(y, w, w_idx, h, mods, layer, g_gate, dims, name):
    m, k = y.shape
    d = h.shape[1]
    t, ctx, nb = dims
    tk = _k_tile(k)
    nk = k // tk
    tm, tn = _mm_tiles(t, tk, d, 1, 8, tn_target=1024 if nk > 1 else 512)
    nt = d // tn
    kern = functools.partial(_residual_kernel, tm=tm, tpb=t // tm, ctx=ctx, nb=nb, nk=nk)
    return pl.pallas_call(
        kern,
        out_shape=jax.ShapeDtypeStruct(h.shape, F32),
        grid=(m // tm, nt, nk),
        in_specs=[pl.BlockSpec((tm, tk), lambda i, j, kk: (i, kk)),
                  pl.BlockSpec((None, tk, tn), lambda i, j, kk: (w_idx, kk, j)),
                  pl.BlockSpec((tm, tn), lambda i, j, kk: (i, j)),
                  pl.BlockSpec((None, MOD_ROWS, tn), lambda i, j, kk: (layer, 0, g_gate * nt + j))],
        out_specs=pl.BlockSpec((tm, tn), lambda i, j, kk: (i, j)),
        scratch_shapes=[pltpu.VMEM((tm, tn), F32)],
        input_output_aliases={2: 0},
        compiler_params=_params(("parallel", "arbitrary", "arbitrary")),
        name=name,
    )(y, w, h, mods)


def _gated_out_kernel(of_ref, ob_ref, g_ref, nw_ref, w_ref, h_ref, m_ref, o_ref, acc_ref, *,
                      dv, heads, nk, tm, tpb, ctx, nb):
    kk = pl.program_id(1)
    nw = nw_ref[...]
    ys = []
    for hh in range(heads):
        cols = slice(hh * dv, (hh + 1) * dv)
        o = of_ref[:, cols] + ob_ref[:, cols]
        y = o * lax.rsqrt(jnp.mean(o * o, axis=-1, keepdims=True) + EPS) * nw
        ys.append((y * _silu(g_ref[:, cols])).astype(BF16))
    part = _dot(ys[0] if heads == 1 else jnp.concatenate(ys, axis=1), w_ref)

    @pl.when(kk == 0)
    def _():
        acc_ref[...] = part

    @pl.when(kk > 0)
    def _():
        acc_ref[...] += part

    @pl.when(kk == nk - 1)
    def _():
        gate = _mod_rows(m_ref, pl.program_id(0), tm, tpb, ctx, nb)
        o_ref[...] = h_ref[...] + gate * acc_ref[...]


def _gated_out_call(o_f, o_b, gate_arr, gate_col0, o_norm, w_o, w_idx, h, mods, layer, g_gate, dims, name):
    m, width = o_f.shape
    d = h.shape[1]
    t, ctx, nb = dims
    dv = o_norm.shape[-1]
    tk = min(width, max(dv, 512))
    nk = width // tk
    tm = _row_tile(t, 544)
    goff = gate_col0 // tk
    kern = functools.partial(_gated_out_kernel, dv=dv, heads=tk // dv, nk=nk, tm=tm, tpb=t // tm,
                             ctx=ctx, nb=nb)
    return pl.pallas_call(
        kern,
        out_shape=jax.ShapeDtypeStruct(h.shape, F32),
        grid=(m // tm, nk),
        in_specs=[pl.BlockSpec((tm, tk), lambda i, kk: (i, kk)),
                  pl.BlockSpec((tm, tk), lambda i, kk: (i, kk)),
                  pl.BlockSpec((tm, tk), lambda i, kk: (i, goff + kk)),
                  pl.BlockSpec((None, 1, dv), lambda i, kk: (w_idx, 0, 0)),
                  pl.BlockSpec((None, tk, d), lambda i, kk: (w_idx, kk, 0)),
                  pl.BlockSpec((tm, d), lambda i, kk: (i, 0), pipeline_mode=pl.Buffered(1)),
                  pl.BlockSpec((None, MOD_ROWS, d), lambda i, kk: (layer, 0, g_gate))],
        out_specs=pl.BlockSpec((tm, d), lambda i, kk: (i, 0)),
        scratch_shapes=[pltpu.VMEM((tm, d), F32)],
        input_output_aliases={5: 0},
        compiler_params=_params(("parallel", "arbitrary")),
        name=name,
    )(o_f, o_b, gate_arr, o_norm.reshape(o_norm.shape[0], 1, dv), w_o, h, mods)


def _chunk_cumsum(g, c, reverse):
    n = g.shape[0]
    pos = jnp.bitwise_and(lax.broadcasted_iota(jnp.int32, (n, 1), 0), c - 1)
    s = 1
    while s < c:
        if reverse:
            g = g + jnp.where(pos < c - s, pltpu.roll(g, n - s, 0), 0.0)
        else:
            g = g + jnp.where(pos >= s, pltpu.roll(g, s, 0), 0.0)
        s *= 2
    return g


def _bwd_group(s, ng):
    return jnp.where(s == 0, 0, ng - s)


def _dot_nt(a, b):
    return lax.dot_general(a, b, (((1,), (1,)), ((), ())), preferred_element_type=F32)


def _gla_scan_kernel(qf_ref, kf_ref, vf_ref, gf_ref, qb_ref, kb_ref, vb_ref, gb_ref,
                     of_ref, ob_ref, st_ref, b_ref, *, gt, dk, dv):
    c = CHUNK
    nch = gt // c

    @pl.when(pl.program_id(2) == 0)
    def _():
        st_ref[...] = jnp.zeros_like(st_ref)

    crow = lax.broadcasted_iota(jnp.int32, (c, c), 0)
    ccol = lax.broadcasted_iota(jnp.int32, (c, c), 1)
    masks = (ccol <= crow, ccol >= crow)
    mid = (c // 2 - 1, c // 2)
    end = (c - 1, 0)
    g_refs = (gf_ref, gb_ref)
    q_refs = (qf_ref, qb_ref)
    k_refs = (kf_ref, kb_ref)
    v_refs = (vf_ref, vb_ref)
    o_refs = (of_ref, ob_ref)

    worst = jnp.zeros((1, dk), F32)
    mag = jnp.ones((1, dk), F32)
    for d in range(2):
        mag = jnp.maximum(mag, jnp.max(jnp.maximum(jnp.abs(q_refs[d][...]), jnp.abs(k_refs[d][...])),
                                       axis=0, keepdims=True))
        b = _chunk_cumsum(g_refs[d][...], c, reverse=d == 1)
        b_ref[d] = b
        for ci in range(nch):
            r = b[ci * c + mid[d]:ci * c + mid[d] + 1, :]
            e = b[ci * c + end[d]:ci * c + end[d] + 1, :]
            worst = jnp.maximum(worst, jnp.maximum(-r, r - e))
    safe = jnp.max(worst + jnp.log(mag)) <= EXP_RANGE

    def scores_fast(d, ci, q, k, b):
        r = b[mid[d]:mid[d] + 1, :]
        qs = (q * jnp.exp(b - r)).astype(BF16)
        ks = (k * jnp.exp(r - b)).astype(BF16)
        return jnp.where(masks[d], _dot_nt(qs, ks), 0.0)

    def scores_exact(d, ci, q, k, b):
        lane = lax.broadcasted_iota(jnp.int32, (c, c), 1)

        def body(s, acc):
            ks = k_refs[d][pl.ds(ci * c + s, 1), :]
            bs = b_ref[d, pl.ds(ci * c + s, 1), :]
            term = q * ks * jnp.exp(jnp.minimum(b - bs, 0.0))
            return jnp.where(lane == s, jnp.sum(term, axis=1, keepdims=True), acc)

        acc = lax.fori_loop(0, c, body, jnp.zeros((c, c), F32))
        return jnp.where(masks[d], acc, 0.0)

    def run(scores):
        pre = {}
        for d in range(2):
            for ci in range(nch):
                rows = slice(ci * c, (ci + 1) * c)
                q = q_refs[d][rows, :]
                k = k_refs[d][rows, :]
                v = v_refs[d][rows, :]
                b = b_ref[d, rows, :]
                b_end = b[end[d]:end[d] + 1, :]
                a = scores(d, ci, q, k, b).astype(BF16)
                ke = (k * jnp.exp(b_end - b)).astype(BF16)
                pre[d, ci] = dict(
                    av=jnp.dot(a, v.astype(BF16), preferred_element_type=F32),
                    qe=(q * jnp.exp(b)).astype(BF16),
                    upd=jnp.dot(v.T.astype(BF16), ke, preferred_element_type=F32),
                    dec=jnp.exp(b_end))
        for d in range(2):
            st = st_ref[d]
            for step in range(nch):
                ci = step if d == 0 else nch - 1 - step
                p = pre[d, ci]
                o_refs[d][ci * c:(ci + 1) * c, :] = p["av"] + _dot_nt(p["qe"], st.astype(BF16))
                st = st * p["dec"] + p["upd"]
            st_ref[d] = st

    @pl.when(safe)
    def _():
        run(scores_fast)

    @pl.when(jnp.logical_not(safe))
    def _():
        run(scores_exact)


def _gla_scan_call(q_src, k_srcs, v_src, g_arr, nb, t, gt, heads, dk, dv):
    ng = t // gt

    def spec3(col0, width, bwd):
        off = col0 // width
        if bwd:
            return pl.BlockSpec((None, gt, width), lambda b, h, s: (b, _bwd_group(s, ng), off + h))
        return pl.BlockSpec((None, gt, width), lambda b, h, s: (b, s, off + h))

    def spec4(lead, col0, width, bwd):
        off = col0 // width
        if bwd:
            return pl.BlockSpec((None, None, gt, width),
                                lambda b, h, s: (lead, b, _bwd_group(s, ng), off + h))
        return pl.BlockSpec((None, None, gt, width), lambda b, h, s: (lead, b, s, off + h))

    def kspec(src, bwd):
        arr, lead, col0 = src
        return spec3(col0, dk, bwd) if lead is None else spec4(lead, col0, dk, bwd)

    in_specs, args = [], []
    for d in range(2):
        bwd = d == 1
        in_specs += [spec3(q_src[1], dk, bwd), kspec(k_srcs[d], bwd), spec3(v_src[1], dv, bwd),
                     spec4(d, 0, dk, bwd)]
        args += [q_src[0], k_srcs[d][0], v_src[0], g_arr]
    out = jax.ShapeDtypeStruct((nb, t, heads * dv), F32)
    kern = functools.partial(_gla_scan_kernel, gt=gt, dk=dk, dv=dv)
    return pl.pallas_call(
        kern,
        out_shape=(out, out),
        grid=(nb, heads, ng),
        in_specs=in_specs,
        out_specs=(pl.BlockSpec((None, gt, dv), lambda b, h, s: (b, s, h)),
                   pl.BlockSpec((None, gt, dv), lambda b, h, s: (b, _bwd_group(s, ng), h))),
        scratch_shapes=[pltpu.VMEM((2, dv, dk), F32), pltpu.VMEM((2, gt, dk), F32)],
        compiler_params=_params(("parallel", "parallel", "arbitrary")),
        name="gla_scan",
    )(*args)


def _dn_conv_kernel(x_ref, w_ref, o_ref, *, gt, n_norm_tiles, n_q_tiles, qscale):
    s = pl.program_id(1)
    j = pl.program_id(2)
    x = x_ref[...]
    w = w_ref[...]
    seg = jnp.where(s == 0, gt, GRID_W)
    r = lax.broadcasted_iota(jnp.int32, (gt, 1), 0)
    pos = jnp.where(s == 0, r, jnp.bitwise_and(r, GRID_W - 1))
    acc = x * w[CONV_K // 2:CONV_K // 2 + 1, :]
    for tap in range(CONV_K):
        off = tap - CONV_K // 2
        if off == 0:
            continue
        shifted = pltpu.roll(x, (-off) % gt, 0)
        ok = (pos + off >= 0) & (pos + off < seg)
        acc = acc + jnp.where(ok, shifted, 0.0) * w[tap:tap + 1, :]
    y = _silu(acc)
    o_ref[...] = y

    @pl.when(j < n_norm_tiles)
    def _():
        scale = jnp.where(j < n_q_tiles, qscale, 1.0)
        for hh in range(y.shape[1] // HEAD_DIM):
            cols = slice(hh * HEAD_DIM, (hh + 1) * HEAD_DIM)
            yh = y[:, cols]
            o_ref[:, cols] = yh * (lax.rsqrt(jnp.sum(yh * yh, axis=-1, keepdims=True) + EPS) * scale)


def _dn_conv_call(p, conv_w, layer_idx, nb, t, gt, qkv_w, key_w):
    tc = min(512, key_w)
    kern = functools.partial(_dn_conv_kernel, gt=gt, n_norm_tiles=2 * key_w // tc,
                             n_q_tiles=key_w // tc, qscale=HEAD_DIM ** -0.5)
    return pl.pallas_call(
        kern,
        out_shape=jax.ShapeDtypeStruct((nb, t, qkv_w), F32),
        grid=(nb, t // gt, qkv_w // tc),
        in_specs=[pl.BlockSpec((None, gt, tc), lambda b, s, j: (b, s, j)),
                  pl.BlockSpec((None, CONV_K, tc), lambda b, s, j: (layer_idx, 0, j))],
        out_specs=pl.BlockSpec((None, gt, tc), lambda b, s, j: (b, s, j)),
        compiler_params=_params(("parallel", "parallel", "parallel")),
        name="deltanet_conv",
    )(p, conv_w)


def _mm3(a, b):
    a1 = a.astype(BF16)
    a2 = (a - a1.astype(F32)).astype(BF16)
    b1 = b.astype(BF16)
    b2 = (b - b1.astype(F32)).astype(BF16)
    return (jnp.dot(a1, b1, preferred_element_type=F32) + jnp.dot(a1, b2, preferred_element_type=F32)
            + jnp.dot(a2, b1, preferred_element_type=F32))


def _mm1(a, b):
    return jnp.dot(a.astype(BF16), b.astype(BF16), preferred_element_type=F32)


INVERSE_EXACT_LEVELS = 1


def _neumann_inverses(ms, eye):
    c = CHUNK
    ps = [eye + m for m in ms]
    xs = [_mm3(m, m) for m in ms]
    levels = c.bit_length() - 2
    for lvl in range(levels):
        mm = _mm3 if lvl < INVERSE_EXACT_LEVELS else _mm1
        if lvl == levels - 1:
            ps = [p + mm(p, x) for p, x in zip(ps, xs)]
        else:
            rs = [mm(jnp.concatenate([x, p], axis=0), x) for p, x in zip(ps, xs)]
            xs = [r[:c] for r in rs]
            ps = [p + r[c:] for p, r in zip(ps, rs)]
    return ps


def _dn_scan_kernel(qf_ref, kf_ref, vf_ref, qb_ref, kb_ref, vb_ref, beta_f_ref, ld_f_ref,
                    beta_b_ref, ld_b_ref, of_ref, ob_ref, st_ref, *, gt, rep):
    c = CHUNK
    nch = gt // c

    @pl.when(pl.program_id(2) == 0)
    def _():
        st_ref[...] = jnp.zeros_like(st_ref)

    row = lax.broadcasted_iota(jnp.int32, (c, c), 0)
    col = lax.broadcasted_iota(jnp.int32, (c, c), 1)
    diag = row == col
    eye = jnp.where(diag, 1.0, 0.0)
    incl = (col <= row, col >= row)
    strict = (col < row, col > row)
    end = (c - 1, 0)
    q_refs = (qf_ref, qb_ref)
    k_refs = (kf_ref, kb_ref)
    v_refs = (vf_ref, vb_ref)
    o_refs = (of_ref, ob_ref)
    beta_refs = (beta_f_ref, beta_b_ref)
    ld_refs = (ld_f_ref, ld_b_ref)

    def to_col(x_row):
        return jnp.sum(jnp.where(diag, x_row, 0.0), axis=1, keepdims=True)

    units = {}
    for d in range(2):
        for ci in range(nch):
            rows = slice(ci * c, (ci + 1) * c)
            q = q_refs[d][rows, :]
            k = k_refs[d][rows, :]
            qb16 = q.astype(BF16)
            kb16 = k.astype(BF16)
            kk = _dot_nt(kb16, kb16)
            qk = _dot_nt(qb16, kb16)
            for r in range(rep):
                beta_row = beta_refs[d][r, ci:ci + 1, :]
                g_row = ld_refs[d][r, ci:ci + 1, :]
                g_col = to_col(g_row)
                beta_col = to_col(beta_row)
                gc_col = jnp.sum(jnp.where(incl[d], g_row, 0.0), axis=1, keepdims=True)
                gc_row = jnp.sum(jnp.where(incl[1 - d], g_col, 0.0), axis=0, keepdims=True)
                dec = jnp.exp(jnp.minimum(gc_col - gc_row, 0.0))
                gc_end = gc_col[end[d]:end[d] + 1, :]
                eg = jnp.exp(gc_col)
                v = v_refs[d][rows, r * HEAD_DIM:(r + 1) * HEAD_DIM]
                units[d, ci, r] = dict(
                    m=jnp.where(strict[d], -(beta_col * kk * dec), 0.0),
                    rhs_v=(beta_col * v).astype(BF16),
                    rhs_k=(beta_col * eg * k).astype(BF16),
                    a=jnp.where(incl[d], qk * dec, 0.0).astype(BF16),
                    kd_t=(k * jnp.exp(gc_end - gc_col)).T.astype(BF16),
                    eg=eg, dec_end=jnp.exp(gc_end), qb16=qb16)

    keys = list(units)
    for key, tinv in zip(keys, _neumann_inverses([units[key]["m"] for key in keys], eye)):
        u = units[key]
        sol = jnp.dot(tinv.astype(BF16), jnp.concatenate([u["rhs_v"], u["rhs_k"]], axis=1),
                      preferred_element_type=F32)
        u["sol_v"] = sol[:, :HEAD_DIM]
        u["kq"] = jnp.concatenate([sol[:, HEAD_DIM:].astype(BF16), u["qb16"]], axis=0)
        u["a_kd"] = jnp.concatenate([u["a"], u["kd_t"]], axis=0)

    chains = [(d, r) for d in range(2) for r in range(rep)]
    states = {ch: st_ref[ch[0], ch[1]] for ch in chains}
    for step in range(nch):
        for d, r in chains:
            ci = step if d == 0 else nch - 1 - step
            u = units[d, ci, r]
            st = states[d, r]
            ks_qs = jnp.dot(u["kq"], st.astype(BF16), preferred_element_type=F32)
            w16 = (u["sol_v"] - ks_qs[:c]).astype(BF16)
            aw_kw = jnp.dot(u["a_kd"], w16, preferred_element_type=F32)
            o_refs[d][ci * c:(ci + 1) * c, r * HEAD_DIM:(r + 1) * HEAD_DIM] = u["eg"] * ks_qs[c:] + aw_kw[:c]
            states[d, r] = st * u["dec_end"] + aw_kw[c:]
    for d, r in chains:
        st_ref[d, r] = states[d, r]


def _dn_scan_call(qkv, beta, ld, nb, t, gt, key_w, rep):
    ng = t // gt
    hd = HEAD_DIM
    hq = key_w // hd
    nch = gt // CHUNK
    koff = key_w // hd
    voff = 2 * key_w // (rep * hd)

    def spec(width, off, bwd):
        if bwd:
            return pl.BlockSpec((None, gt, width), lambda b, h, s: (b, _bwd_group(s, ng), off + h))
        return pl.BlockSpec((None, gt, width), lambda b, h, s: (b, s, off + h))

    def sspec(d):
        if d == 1:
            return pl.BlockSpec((None, None, rep, None, nch, CHUNK),
                                lambda b, h, s: (1, b, h, _bwd_group(s, ng), 0, 0))
        return pl.BlockSpec((None, None, rep, None, nch, CHUNK), lambda b, h, s: (0, b, h, s, 0, 0))

    in_specs, args = [], []
    for d in range(2):
        in_specs += [spec(hd, 0, d == 1), spec(hd, koff, d == 1), spec(rep * hd, voff, d == 1)]
        args += [qkv, qkv, qkv]
    in_specs += [sspec(0), sspec(0), sspec(1), sspec(1)]
    args += [beta, ld, beta, ld]
    out = jax.ShapeDtypeStruct((nb, t, hq * rep * hd), F32)
    kern = functools.partial(_dn_scan_kernel, gt=gt, rep=rep)
    return pl.pallas_call(
        kern,
        out_shape=(out, out),
        grid=(nb, hq, ng),
        in_specs=in_specs,
        out_specs=(pl.BlockSpec((None, gt, rep * hd), lambda b, h, s: (b, s, h)),
                   pl.BlockSpec((None, gt, rep * hd), lambda b, h, s: (b, _bwd_group(s, ng), h))),
        scratch_shapes=[pltpu.VMEM((2, rep, hd, hd), F32)],
        compiler_params=_params(("parallel", "parallel", "arbitrary")),
        name="deltanet_scan",
    )(*args)


def _final_norm_kernel(h_ref, w_ref, o_ref):
    x = h_ref[...]
    o_ref[...] = x * lax.rsqrt(jnp.mean(x * x, axis=-1, keepdims=True) + EPS) * w_ref[...]


def _final_norm_call(h3, w, ctx, seq):
    nb, t, d = h3.shape
    skip = ctx // ctx
    return pl.pallas_call(
        _final_norm_kernel,
        out_shape=jax.ShapeDtypeStruct((nb, seq, d), F32),
        grid=(nb, seq // ctx),
        in_specs=[pl.BlockSpec((None, ctx, d), lambda b, i: (b, i + skip, 0)),
                  pl.BlockSpec((1, d), lambda b, i: (0, 0))],
        out_specs=pl.BlockSpec((None, ctx, d), lambda b, i: (b, i, 0)),
        compiler_params=_params(("parallel", "parallel")),
        name="final_norm",
    )(h3, w.reshape(1, d))


def _scan_scalars(x, nb, t, gt):
    hv = x.shape[-1]
    x = x.reshape(2, nb, t, hv).transpose(0, 1, 3, 2)
    return x.reshape(2, nb, hv, t // gt, gt // CHUNK, CHUNK)


def kernel(x, c, ctx, c_ctx, w_ada, b_ada, norm_mix, norm_ffn, hg_lb_logits, hg_w_in, hg_w_f, hg_o_norm, hg_w_o, gla_w_in, gla_w_g1, gla_w_g2, gla_b_g, gla_o_norm, gla_w_o, dn_w_in, dn_conv, dn_w_b, dn_w_a, dn_a_log, dn_dt_bias, dn_o_norm, dn_w_o, ffn_w_gate, ffn_w_up, ffn_w_down, final_norm):
    nb, seq, d = x.shape
    n_ctx = ctx.shape[1]
    depth = w_ada.shape[0]
    t = n_ctx + seq
    gt = n_ctx
    assert n_ctx % CHUNK == 0 and seq % gt == 0 and GRID_W == CHUNK and nb + 1 <= MOD_ROWS
    dims = (t, n_ctx, nb)
    m = nb * t

    p = jax.nn.softmax(hg_lb_logits.astype(F32), axis=0)
    lower_bounds = jnp.cumsum(p, axis=0) - p[0]

    cc = jnp.zeros((MOD_ROWS, d), F32).at[:nb].set(c).at[nb].set(c_ctx)
    mods = _ada_call(cc, w_ada, b_ada)

    h = jnp.concatenate([ctx, x], axis=1).reshape(m, d)

    hg_heads = d // HEAD_DIM
    hg_kw = hg_w_f.shape[-1]
    gla_kw = gla_w_g2.shape[-1]
    gla_dk = gla_kw // GLA_HEADS
    gla_dv = d // GLA_HEADS
    dn_key_w = d
    dn_hv = dn_w_b.shape[-1]
    dn_val_w = dn_hv * HEAD_DIM
    dn_qkv_w = 2 * dn_key_w + dn_val_w
    rep = dn_hv // (dn_key_w // HEAD_DIM)

    for i in range(depth):
        kind, j = i % 3, i // 3
        src = _NormSrc(h, norm_mix.reshape(depth, 1, d), mods, i, 0, 1, dims)
        if kind == 0:
            body = functools.partial(_hg_in_body, q_cols=hg_kw, scale=HEAD_DIM ** -0.5)
            proj = _proj_call(body, src, hg_w_in, j, "hgrn2_in_proj", hg_kw)
            kk, lf = _hg_f_call(src, hg_w_f, j, lower_bounds[i])
            p3 = proj.reshape(nb, t, -1)
            kk4 = kk.reshape(2, nb, t, hg_kw)
            o_f, o_b = _gla_scan_call((p3, 0), ((kk4, 0, 0), (kk4, 1, 0)), (p3, hg_kw),
                                      lf.reshape(2, nb, t, hg_kw), nb, t, gt, hg_heads, HEAD_DIM,
                                      d // hg_heads)
            h = _gated_out_call(o_f.reshape(m, -1), o_b.reshape(m, -1), proj, hg_kw + d, hg_o_norm, hg_w_o, j,
                                h, mods, i, 2, dims, "hgrn2_out_proj")
        elif kind == 1:
            body = functools.partial(_gla_in_body, q_cols=gla_kw, scale=gla_dk ** -0.5)
            proj = _proj_call(body, src, gla_w_in, j, "gla_in_proj", gla_kw)
            lg = _gla_gate_call(src, gla_w_g1, gla_w_g2, gla_b_g, j)
            p3 = proj.reshape(nb, t, -1)
            o_f, o_b = _gla_scan_call((p3, 0), ((p3, None, gla_kw), (p3, None, gla_kw)), (p3, 2 * gla_kw),
                                      lg.reshape(2, nb, t, gla_kw), nb, t, gt, GLA_HEADS, gla_dk, gla_dv)
            h = _gated_out_call(o_f.reshape(m, -1), o_b.reshape(m, -1), proj, 2 * gla_kw + d, gla_o_norm,
                                gla_w_o, j, h, mods, i, 2, dims, "gla_out_proj")
        else:
            proj = _proj_call(_plain_body, src, dn_w_in, j, "deltanet_in_proj")
            beta, ld = _dn_bg_call(src, dn_w_b, dn_w_a, dn_a_log, dn_dt_bias, j)
            qkv = _dn_conv_call(proj.reshape(nb, t, -1), dn_conv, j, nb, t, gt, dn_qkv_w, dn_key_w)
            o_f, o_b = _dn_scan_call(qkv, _scan_scalars(beta, nb, t, gt), _scan_scalars(ld, nb, t, gt),
                                     nb, t, gt, dn_key_w, rep)
            h = _gated_out_call(o_f.reshape(m, -1), o_b.reshape(m, -1), proj, dn_qkv_w, dn_o_norm, dn_w_o, j,
                                h, mods, i, 2, dims, "deltanet_out_proj")
        src = _NormSrc(h, norm_ffn.reshape(depth, 1, d), mods, i, 3, 4, dims)
        u = _swiglu_call(src, ffn_w_gate, ffn_w_up, i)
        h = _residual_call(u, ffn_w_down, i, h, mods, i, 5, dims, "ffn_down_proj")

    return _final_norm_call(h.reshape(nb, t, d), final_norm, n_ctx, seq)
```

```python
import functools
from typing import NamedTuple

import jax
import jax.numpy as jnp
from jax import lax
from jax.experimental import pallas as pl
from jax.experimental.pallas import tpu as pltpu

F32 = jnp.float32
BF16 = jnp.bfloat16

EPS = 1e-6
GATE_CLIP = 30.0
CHUNK = 64
GRID_W = 64
CONV_K = 5
GLA_HEADS = 4
GLA_GATE_NORM = 16.0
HEAD_DIM = 128
EXP_RANGE = 80.0

V7X_VMEM_LIMIT = 56 * 1024 * 1024
VMEM_BLOCK_BUDGET = 44 * 1024 * 1024
BF16_SUBLANES = 16
MOD_ROWS = 8


def _params(sem):
    return pltpu.CompilerParams(dimension_semantics=sem, vmem_limit_bytes=V7X_VMEM_LIMIT)


def _sigmoid(x):
    return 1.0 / (1.0 + jnp.exp(-x))


def _silu(x):
    return x * _sigmoid(x)


def _softplus(x):
    return jnp.maximum(x, 0.0) + jnp.log1p(jnp.exp(-jnp.abs(x)))


def _log_sigmoid(x):
    return -_softplus(-x)


def _row_tile(t, target):
    best = 8
    for d in range(8, min(t, target) + 1, 8):
        if t % d == 0:
            best = d
    return best


def _col_tile(n, target):
    for c in (target, 512, 256, 128):
        if c <= target and n % c == 0:
            return c
    return n


def _mod_rows(m_ref, i, tm, tiles_per_batch, ctx, nb):
    b = i // tiles_per_batch
    r0 = (i % tiles_per_batch) * tm
    lat = m_ref[pl.ds(b, 1), :]
    cx = m_ref[nb:nb + 1, :]
    rows = lax.broadcasted_iota(jnp.int32, (tm, 1), 0) + r0
    return jnp.where(rows < ctx, cx, lat)


def _ada_kernel(c_ref, w_ref, b_ref, o_ref):
    s = _silu(c_ref[...]).astype(BF16)
    o_ref[...] = jnp.dot(s, w_ref[...].astype(BF16), preferred_element_type=F32) + b_ref[...]


def _ada_call(cc, w_ada, b_ada):
    depth, d, n = w_ada.shape
    tn = _col_tile(n, 1024)
    return pl.pallas_call(
        _ada_kernel,
        out_shape=jax.ShapeDtypeStruct((depth, MOD_ROWS, n), F32),
        grid=(depth, n // tn),
        in_specs=[pl.BlockSpec((MOD_ROWS, d), lambda i, j: (0, 0)),
                  pl.BlockSpec((None, d, tn), lambda i, j: (i, 0, j)),
                  pl.BlockSpec((None, 1, tn), lambda i, j: (i, 0, j))],
        out_specs=pl.BlockSpec((None, MOD_ROWS, tn), lambda i, j: (i, 0, j)),
        compiler_params=_params(("parallel", "parallel")),
        name="ada_mods",
    )(cc, w_ada, b_ada.reshape(depth, 1, n))


class _NormSrc(NamedTuple):
    h: jax.Array
    norm_w: jax.Array
    mods: jax.Array
    layer: int
    g_shift: int
    g_scale: int
    dims: tuple


def _dot(x, w_ref):
    return jnp.dot(x, w_ref[...].astype(BF16), preferred_element_type=F32)


def _row_chunk(tm):
    best = tm
    for rc in range(BF16_SUBLANES, min(tm, 272) + 1, BF16_SUBLANES):
        if tm % rc == 0:
            best = rc
    return best


def _fused(epilogue, src, tm, n_w):
    t, ctx, nb = src.dims
    tpb = t // tm
    rc = _row_chunk(tm)

    def kern(h_ref, nw_ref, sh_ref, sc_ref, *rest):
        a_ref = rest[-1]
        w_refs = rest[:n_w]
        n_out = epilogue.n_out
        par_refs = rest[n_w:len(rest) - 1 - n_out]
        out_refs = rest[len(rest) - 1 - n_out:-1]
        i = pl.program_id(0)
        b = i // tpb
        r0 = (i % tpb) * tm

        def compute(first):
            ws = [w[...].astype(BF16) for w in w_refs]
            for r in range(tm // rc):
                rows = slice(r * rc, (r + 1) * rc)
                if first:
                    x = h_ref[rows, :]
                    y = x * lax.rsqrt(jnp.mean(x * x, axis=-1, keepdims=True) + EPS) * nw_ref[...]
                    is_ctx = lax.broadcasted_iota(jnp.int32, (rc, 1), 0) + (r0 + r * rc) < ctx
                    scale = jnp.where(is_ctx, sc_ref[nb:nb + 1, :], sc_ref[pl.ds(b, 1), :])
                    shift = jnp.where(is_ctx, sh_ref[nb:nb + 1, :], sh_ref[pl.ds(b, 1), :])
                    a = (y * (1.0 + scale) + shift).astype(BF16)
                    a_ref[rows, :] = a
                else:
                    a = a_ref[rows, :]
                accs = [jnp.dot(a, w, preferred_element_type=F32) for w in ws]
                outs = epilogue(accs, par_refs)
                for o_ref, o in zip(out_refs, outs):
                    o_ref[rows, :] = o.astype(o_ref.dtype)

        @pl.when(pl.program_id(1) == 0)
        def _():
            compute(True)

        @pl.when(pl.program_id(1) > 0)
        def _():
            compute(False)

    return kern


def _src_specs(src, tm):
    d = src.h.shape[1]
    specs = [pl.BlockSpec((tm, d), lambda i, j: (i, 0), pipeline_mode=pl.Buffered(1)),
             pl.BlockSpec((None, 1, d), lambda i, j: (src.layer, 0, 0)),
             pl.BlockSpec((None, MOD_ROWS, d), lambda i, j: (src.layer, 0, src.g_shift)),
             pl.BlockSpec((None, MOD_ROWS, d), lambda i, j: (src.layer, 0, src.g_scale))]
    return specs, [src.h, src.norm_w, src.mods, src.mods]


class _Epilogue:
    def __init__(self, fn, n_out=1):
        self.fn = fn
        self.n_out = n_out

    def __call__(self, accs, par_refs):
        outs = self.fn(accs, par_refs)
        return outs if isinstance(outs, (tuple, list)) else (outs,)


def _fused_call(epilogue, src, tm, grid_inner, n_w, in_specs, args, out_shape, out_specs, name):
    m, d = src.h.shape
    specs, src_args = _src_specs(src, tm)
    return pl.pallas_call(
        _fused(epilogue, src, tm, n_w),
        out_shape=out_shape,
        grid=(m // tm, grid_inner),
        in_specs=specs + in_specs,
        out_specs=out_specs,
        scratch_shapes=[pltpu.VMEM((tm, d), BF16)],
        compiler_params=_params(("parallel", "arbitrary")),
        name=name,
    )(*src_args, *args)


def _mm_tiles(t, k, n, n_w, out_bytes, x_bytes=4, fixed=0, tm_target=1088, tn_target=512):
    tm = _row_tile(t, tm_target)
    while tm > 64:
        for tn in sorted({tn_target, 512, 256}, reverse=True):
            if tn > tn_target:
                continue
            need = tm * k * x_bytes + fixed + 2 * n_w * k * tn * 2 + 2 * tm * tn * out_bytes
            if n % tn == 0 and need <= VMEM_BLOCK_BUDGET:
                return tm, tn
        tm = _row_tile(t, tm - 8)
    return tm, _col_tile(n, 128)


def _fused_tiles(t, k, n, n_w, out_bytes, tn_target=1024):
    return _mm_tiles(t, k, n, n_w, out_bytes, x_bytes=6, fixed=8 * 1024 * 1024, tn_target=tn_target)


def _proj_call(epi_q, src, w, layer_idx, name, q_cols=0):
    m, k = src.h.shape
    n = w.shape[-1]
    tm, tn = _fused_tiles(src.dims[0], k, n, 1, 4)
    while q_cols % tn:
        tn //= 2

    def epi(accs, par_refs):
        acc = accs[0]
        if epi_q is None:
            return acc
        return jnp.where(pl.program_id(1) * tn < q_cols, epi_q(acc), acc)

    return _fused_call(
        _Epilogue(epi), src, tm, n // tn, 1,
        [pl.BlockSpec((None, k, tn), lambda i, j: (layer_idx, 0, j))], [w],
        jax.ShapeDtypeStruct((m, n), F32), pl.BlockSpec((tm, tn), lambda i, j: (i, j)), name)


def _hg_f_epilogue(accs, par_refs):
    z = jnp.clip(accs[0], -GATE_CLIP, GATE_CLIP)
    lb = par_refs[0][...]
    u = jnp.exp(-z)
    r = 1.0 / (1.0 + u)
    return (1.0 - lb) * (u * r), jnp.log((1.0 + lb * u) * r)


def _hg_f_call(src, w_f, layer_idx, lb):
    m, k = src.h.shape
    n = w_f.shape[-1]
    tm, tn = _fused_tiles(src.dims[0], k, n, 1, 8)
    nt = n // tn
    out = jax.ShapeDtypeStruct((2, m, n), F32)
    ospec = pl.BlockSpec((None, tm, tn), lambda i, j: (j // nt, i, j % nt))
    return _fused_call(
        _Epilogue(_hg_f_epilogue, 2), src, tm, 2 * nt, 1,
        [pl.BlockSpec((None, None, k, tn), lambda i, j: (layer_idx, j // nt, 0, j % nt)),
         pl.BlockSpec((1, tn), lambda i, j: (0, j % nt))],
        [w_f, lb.reshape(1, n)], (out, out), (ospec, ospec), "hgrn2_gates")


def _gla_gate_epilogue(accs, par_refs):
    w2_ref, b_ref = par_refs
    gl = jnp.dot(accs[0].astype(BF16), w2_ref[...].astype(BF16), preferred_element_type=F32) + b_ref[...]
    return _log_sigmoid(gl) * (1.0 / GLA_GATE_NORM)


def _gla_gate_call(src, w_g1, w_g2, b_g, layer_idx):
    m, k = src.h.shape
    rank, kw = w_g2.shape[-2:]
    tm = _row_tile(src.dims[0], 544)
    return _fused_call(
        _Epilogue(_gla_gate_epilogue), src, tm, 2, 1,
        [pl.BlockSpec((None, None, k, rank), lambda i, d: (layer_idx, d, 0, 0)),
         pl.BlockSpec((None, None, rank, kw), lambda i, d: (layer_idx, d, 0, 0)),
         pl.BlockSpec((None, None, 1, kw), lambda i, d: (layer_idx, d, 0, 0))],
        [w_g1, w_g2, b_g.reshape(b_g.shape[0], 2, 1, kw)],
        jax.ShapeDtypeStruct((2, m, kw), F32), pl.BlockSpec((None, tm, kw), lambda i, d: (d, i, 0)),
        "gla_gates")


def _dn_bg_epilogue(accs, par_refs):
    alog_ref, dtb_ref = par_refs
    beta = _sigmoid(accs[0])
    dt = _softplus(accs[1] + dtb_ref[...])
    return beta, -jnp.exp(alog_ref[...]) * dt


def _dn_bg_call(src, w_b, w_a, a_log, dt_bias, layer_idx):
    m, k = src.h.shape
    hv = w_b.shape[-1]
    tm = _row_tile(src.dims[0], 544)
    out = jax.ShapeDtypeStruct((2, m, hv), F32)
    wspec = pl.BlockSpec((None, None, k, hv), lambda i, d: (layer_idx, d, 0, 0))
    pspec = pl.BlockSpec((None, None, 1, hv), lambda i, d: (layer_idx, d, 0, 0))
    ospec = pl.BlockSpec((None, tm, hv), lambda i, d: (d, i, 0))
    nl = a_log.shape[0]
    return _fused_call(
        _Epilogue(_dn_bg_epilogue, 2), src, tm, 2, 2, [wspec, wspec, pspec, pspec],
        [w_b, w_a, a_log.reshape(nl, 2, 1, hv), dt_bias.reshape(nl, 2, 1, hv)],
        (out, out), (ospec, ospec), "deltanet_beta_decay")


def _swiglu_epilogue(accs, par_refs):
    return _silu(accs[0]) * accs[1]


def _swiglu_call(src, w_gate, w_up, layer):
    m, k = src.h.shape
    n = w_gate.shape[-1]
    tm, tn = _fused_tiles(src.dims[0], k, n, 2, 2, tn_target=512)
    wspec = pl.BlockSpec((None, k, tn), lambda i, j: (layer, 0, j))
    return _fused_call(
        _Epilogue(_swiglu_epilogue), src, tm, n // tn, 2, [wspec, wspec], [w_gate, w_up],
        jax.ShapeDtypeStruct((m, n), BF16), pl.BlockSpec((tm, tn), lambda i, j: (i, j)), "ffn_swiglu")


def _residual_kernel(x_ref, w_ref, h_ref, g_ref, o_ref, *, tm, tpb, ctx, nb):
    i = pl.program_id(0)
    b = i // tpb
    r0 = (i % tpb) * tm
    rc = _row_chunk(tm)
    w = w_ref[...].astype(BF16)
    for r in range(tm // rc):
        rows = slice(r * rc, (r + 1) * rc)
        is_ctx = lax.broadcasted_iota(jnp.int32, (rc, 1), 0) + (r0 + r * rc) < ctx
        gate = jnp.where(is_ctx, g_ref[nb:nb + 1, :], g_ref[pl.ds(b, 1), :])
        o_ref[rows, :] = h_ref[rows, :] + gate * jnp.dot(x_ref[rows, :], w, preferred_element_type=F32)


def _residual_call(y, w, w_idx, h, mods, layer, g_gate, dims, name):
    m, k = y.shape
    d = h.shape[1]
    t, ctx, nb = dims
    tm, tn = _mm_tiles(t, k, d, 1, 8)
    nt = d // tn
    kern = functools.partial(_residual_kernel, tm=tm, tpb=t // tm, ctx=ctx, nb=nb)
    return pl.pallas_call(
        kern,
        out_shape=jax.ShapeDtypeStruct(h.shape, F32),
        grid=(m // tm, nt),
        in_specs=[pl.BlockSpec((tm, k), lambda i, j: (i, 0)),
                  pl.BlockSpec((None, k, tn), lambda i, j: (w_idx, 0, j)),
                  pl.BlockSpec((tm, tn), lambda i, j: (i, j)),
                  pl.BlockSpec((None, MOD_ROWS, tn), lambda i, j: (layer, 0, g_gate * nt + j))],
        out_specs=pl.BlockSpec((tm, tn), lambda i, j: (i, j)),
        input_output_aliases={2: 0},
        compiler_params=_params(("parallel", "arbitrary")),
        name=name,
    )(y, w, h, mods)


def _gated_out_kernel(of_ref, ob_ref, g_ref, nw_ref, w_ref, h_ref, m_ref, o_ref, acc_ref, *,
                      dv, heads, nk, tm, tpb, ctx, nb):
    kk = pl.program_id(1)
    nw = nw_ref[...]
    ys = []
    for hh in range(heads):
        cols = slice(hh * dv, (hh + 1) * dv)
        o = of_ref[:, cols] + ob_ref[:, cols]
        y = o * lax.rsqrt(jnp.mean(o * o, axis=-1, keepdims=True) + EPS) * nw
        ys.append((y * _silu(g_ref[:, cols])).astype(BF16))
    part = _dot(ys[0] if heads == 1 else jnp.concatenate(ys, axis=1), w_ref)

    @pl.when(kk == 0)
    def _():
        acc_ref[...] = part

    @pl.when(kk > 0)
    def _():
        acc_ref[...] += part

    @pl.when(kk == nk - 1)
    def _():
        gate = _mod_rows(m_ref, pl.program_id(0), tm, tpb, ctx, nb)
        o_ref[...] = h_ref[...] + gate * acc_ref[...]


def _gated_out_call(o_f, o_b, gate_arr, gate_col0, o_norm, w_o, w_idx, h, mods, layer, g_gate, dims, name):
    m, width = o_f.shape
    d = h.shape[1]
    t, ctx, nb = dims
    dv = o_norm.shape[-1]
    tk = min(width, max(dv, 512))
    nk = width // tk
    tm = _row_tile(t, 544)
    goff = gate_col0 // tk
    kern = functools.partial(_gated_out_kernel, dv=dv, heads=tk // dv, nk=nk, tm=tm, tpb=t // tm,
                             ctx=ctx, nb=nb)
    return pl.pallas_call(
        kern,
        out_shape=jax.ShapeDtypeStruct(h.shape, F32),
        grid=(m // tm, nk),
        in_specs=[pl.BlockSpec((tm, tk), lambda i, kk: (i, kk)),
                  pl.BlockSpec((tm, tk), lambda i, kk: (i, kk)),
                  pl.BlockSpec((tm, tk), lambda i, kk: (i, goff + kk)),
                  pl.BlockSpec((None, 1, dv), lambda i, kk: (w_idx, 0, 0)),
                  pl.BlockSpec((None, tk, d), lambda i, kk: (w_idx, kk, 0)),
                  pl.BlockSpec((tm, d), lambda i, kk: (i, 0), pipeline_mode=pl.Buffered(1)),
                  pl.BlockSpec((None, MOD_ROWS, d), lambda i, kk: (layer, 0, g_gate))],
        out_specs=pl.BlockSpec((tm, d), lambda i, kk: (i, 0)),
        scratch_shapes=[pltpu.VMEM((tm, d), F32)],
        input_output_aliases={5: 0},
        compiler_params=_params(("parallel", "arbitrary")),
        name=name,
    )(o_f, o_b, gate_arr, o_norm.reshape(o_norm.shape[0], 1, dv), w_o, h, mods)


def _chunk_cumsum(g, c, reverse):
    n = g.shape[0]
    pos = jnp.bitwise_and(lax.broadcasted_iota(jnp.int32, (n, 1), 0), c - 1)
    s = 1
    while s < c:
        if reverse:
            g = g + jnp.where(pos < c - s, pltpu.roll(g, n - s, 0), 0.0)
        else:
            g = g + jnp.where(pos >= s, pltpu.roll(g, s, 0), 0.0)
        s *= 2
    return g


def _bwd_group(s, ng):
    return jnp.where(s == 0, 0, ng - s)


def _dot_nt(a, b):
    return lax.dot_general(a, b, (((1,), (1,)), ((), ())), preferred_element_type=F32)


def _gla_scan_kernel(qf_ref, kf_ref, vf_ref, gf_ref, qb_ref, kb_ref, vb_ref, gb_ref,
                     of_ref, ob_ref, st_ref, b_ref, *, gt, dk, dv):
    c = CHUNK
    nch = gt // c

    @pl.when(pl.program_id(2) == 0)
    def _():
        st_ref[...] = jnp.zeros_like(st_ref)

    crow = lax.broadcasted_iota(jnp.int32, (c, c), 0)
    ccol = lax.broadcasted_iota(jnp.int32, (c, c), 1)
    masks = (ccol <= crow, ccol >= crow)
    mid = (c // 2 - 1, c // 2)
    end = (c - 1, 0)
    g_refs = (gf_ref, gb_ref)
    q_refs = (qf_ref, qb_ref)
    k_refs = (kf_ref, kb_ref)
    v_refs = (vf_ref, vb_ref)
    o_refs = (of_ref, ob_ref)

    worst = jnp.zeros((1, dk), F32)
    mag = jnp.ones((1, dk), F32)
    for d in range(2):
        mag = jnp.maximum(mag, jnp.max(jnp.maximum(jnp.abs(q_refs[d][...]), jnp.abs(k_refs[d][...])),
                                       axis=0, keepdims=True))
        b = _chunk_cumsum(g_refs[d][...], c, reverse=d == 1)
        b_ref[d] = b
        for ci in range(nch):
            r = b[ci * c + mid[d]:ci * c + mid[d] + 1, :]
            e = b[ci * c + end[d]:ci * c + end[d] + 1, :]
            worst = jnp.maximum(worst, jnp.maximum(-r, r - e))
    safe = jnp.max(worst + jnp.log(mag)) <= EXP_RANGE

    def scores_fast(d, ci, q, k, b):
        r = b[mid[d]:mid[d] + 1, :]
        qs = (q * jnp.exp(b - r)).astype(BF16)
        ks = (k * jnp.exp(r - b)).astype(BF16)
        return jnp.where(masks[d], _dot_nt(qs, ks), 0.0)

    def scores_exact(d, ci, q, k, b):
        lane = lax.broadcasted_iota(jnp.int32, (c, c), 1)

        def body(s, acc):
            ks = k_refs[d][pl.ds(ci * c + s, 1), :]
            bs = b_ref[d, pl.ds(ci * c + s, 1), :]
            term = q * ks * jnp.exp(jnp.minimum(b - bs, 0.0))
            return jnp.where(lane == s, jnp.sum(term, axis=1, keepdims=True), acc)

        acc = lax.fori_loop(0, c, body, jnp.zeros((c, c), F32))
        return jnp.where(masks[d], acc, 0.0)

    def run(scores):
        pre = {}
        for d in range(2):
            for ci in range(nch):
                rows = slice(ci * c, (ci + 1) * c)
                q = q_refs[d][rows, :]
                k = k_refs[d][rows, :]
                v = v_refs[d][rows, :]
                b = b_ref[d, rows, :]
                b_end = b[end[d]:end[d] + 1, :]
                a = scores(d, ci, q, k, b).astype(BF16)
                ke = (k * jnp.exp(b_end - b)).astype(BF16)
                pre[d, ci] = dict(
                    av=jnp.dot(a, v.astype(BF16), preferred_element_type=F32),
                    qe=(q * jnp.exp(b)).astype(BF16),
                    upd=jnp.dot(v.T.astype(BF16), ke, preferred_element_type=F32),
                    dec=jnp.exp(b_end))
        for d in range(2):
            st = st_ref[d]
            for step in range(nch):
                ci = step if d == 0 else nch - 1 - step
                p = pre[d, ci]
                o_refs[d][ci * c:(ci + 1) * c, :] = p["av"] + _dot_nt(p["qe"], st.astype(BF16))
                st = st * p["dec"] + p["upd"]
            st_ref[d] = st

    @pl.when(safe)
    def _():
        run(scores_fast)

    @pl.when(jnp.logical_not(safe))
    def _():
        run(scores_exact)


def _gla_scan_call(q_src, k_srcs, v_src, g_arr, nb, t, gt, heads, dk, dv):
    ng = t // gt

    def spec3(col0, width, bwd):
        off = col0 // width
        if bwd:
            return pl.BlockSpec((None, gt, width), lambda b, h, s: (b, _bwd_group(s, ng), off + h))
        return pl.BlockSpec((None, gt, width), lambda b, h, s: (b, s, off + h))

    def spec4(lead, col0, width, bwd):
        off = col0 // width
        if bwd:
            return pl.BlockSpec((None, None, gt, width),
                                lambda b, h, s: (lead, b, _bwd_group(s, ng), off + h))
        return pl.BlockSpec((None, None, gt, width), lambda b, h, s: (lead, b, s, off + h))

    def kspec(src, bwd):
        arr, lead, col0 = src
        return spec3(col0, dk, bwd) if lead is None else spec4(lead, col0, dk, bwd)

    in_specs, args = [], []
    for d in range(2):
        bwd = d == 1
        in_specs += [spec3(q_src[1], dk, bwd), kspec(k_srcs[d], bwd), spec3(v_src[1], dv, bwd),
                     spec4(d, 0, dk, bwd)]
        args += [q_src[0], k_srcs[d][0], v_src[0], g_arr]
    out = jax.ShapeDtypeStruct((nb, t, heads * dv), F32)
    kern = functools.partial(_gla_scan_kernel, gt=gt, dk=dk, dv=dv)
    return pl.pallas_call(
        kern,
        out_shape=(out, out),
        grid=(nb, heads, ng),
        in_specs=in_specs,
        out_specs=(pl.BlockSpec((None, gt, dv), lambda b, h, s: (b, s, h)),
                   pl.BlockSpec((None, gt, dv), lambda b, h, s: (b, _bwd_group(s, ng), h))),
        scratch_shapes=[pltpu.VMEM((2, dv, dk), F32), pltpu.VMEM((2, gt, dk), F32)],
        compiler_params=_params(("parallel", "parallel", "arbitrary")),
        name="gla_scan",
    )(*args)


def _dn_conv_kernel(x_ref, w_ref, o_ref, *, gt, n_norm_tiles, n_q_tiles, qscale):
    s = pl.program_id(1)
    j = pl.program_id(2)
    x = x_ref[...]
    w = w_ref[...]
    seg = jnp.where(s == 0, gt, GRID_W)
    r = lax.broadcasted_iota(jnp.int32, (gt, 1), 0)
    pos = jnp.where(s == 0, r, jnp.bitwise_and(r, GRID_W - 1))
    acc = x * w[CONV_K // 2:CONV_K // 2 + 1, :]
    for tap in range(CONV_K):
        off = tap - CONV_K // 2
        if off == 0:
            continue
        shifted = pltpu.roll(x, (-off) % gt, 0)
        ok = (pos + off >= 0) & (pos + off < seg)
        acc = acc + jnp.where(ok, shifted, 0.0) * w[tap:tap + 1, :]
    y = _silu(acc)
    o_ref[...] = y

    @pl.when(j < n_norm_tiles)
    def _():
        scale = jnp.where(j < n_q_tiles, qscale, 1.0)
        for hh in range(y.shape[1] // HEAD_DIM):
            cols = slice(hh * HEAD_DIM, (hh + 1) * HEAD_DIM)
            yh = y[:, cols]
            o_ref[:, cols] = yh * (lax.rsqrt(jnp.sum(yh * yh, axis=-1, keepdims=True) + EPS) * scale)


def _dn_conv_call(p, conv_w, layer_idx, nb, t, gt, qkv_w, key_w):
    tc = min(512, key_w)
    kern = functools.partial(_dn_conv_kernel, gt=gt, n_norm_tiles=2 * key_w // tc,
                             n_q_tiles=key_w // tc, qscale=HEAD_DIM ** -0.5)
    return pl.pallas_call(
        kern,
        out_shape=jax.ShapeDtypeStruct((nb, t, qkv_w), F32),
        grid=(nb, t // gt, qkv_w // tc),
        in_specs=[pl.BlockSpec((None, gt, tc), lambda b, s, j: (b, s, j)),
                  pl.BlockSpec((None, CONV_K, tc), lambda b, s, j: (layer_idx, 0, j))],
        out_specs=pl.BlockSpec((None, gt, tc), lambda b, s, j: (b, s, j)),
        compiler_params=_params(("parallel", "parallel", "parallel")),
        name="deltanet_conv",
    )(p, conv_w)


def _mm3(a, b):
    a1 = a.astype(BF16)
    a2 = (a - a1.astype(F32)).astype(BF16)
    b1 = b.astype(BF16)
    b2 = (b - b1.astype(F32)).astype(BF16)
    return (jnp.dot(a1, b1, preferred_element_type=F32) + jnp.dot(a1, b2, preferred_element_type=F32)
            + jnp.dot(a2, b1, preferred_element_type=F32))


def _mm1(a, b):
    return jnp.dot(a.astype(BF16), b.astype(BF16), preferred_element_type=F32)


INVERSE_EXACT_LEVELS = 1


def _neumann_inverses(ms, eye):
    c = CHUNK
    ps = [eye + m for m in ms]
    xs = [_mm3(m, m) for m in ms]
    levels = c.bit_length() - 2
    for lvl in range(levels):
        mm = _mm3 if lvl < INVERSE_EXACT_LEVELS else _mm1
        if lvl == levels - 1:
            ps = [p + mm(p, x) for p, x in zip(ps, xs)]
        else:
            rs = [mm(jnp.concatenate([x, p], axis=0), x) for p, x in zip(ps, xs)]
            xs = [r[:c] for r in rs]
            ps = [p + r[c:] for p, r in zip(ps, rs)]
    return ps


def _dn_scan_kernel(qf_ref, kf_ref, vf_ref, qb_ref, kb_ref, vb_ref, beta_f_ref, ld_f_ref,
                    beta_b_ref, ld_b_ref, of_ref, ob_ref, st_ref, *, gt, rep):
    c = CHUNK
    nch = gt // c

    @pl.when(pl.program_id(2) == 0)
    def _():
        st_ref[...] = jnp.zeros_like(st_ref)

    row = lax.broadcasted_iota(jnp.int32, (c, c), 0)
    col = lax.broadcasted_iota(jnp.int32, (c, c), 1)
    diag = row == col
    eye = jnp.where(diag, 1.0, 0.0)
    incl = (col <= row, col >= row)
    strict = (col < row, col > row)
    end = (c - 1, 0)
    q_refs = (qf_ref, qb_ref)
    k_refs = (kf_ref, kb_ref)
    v_refs = (vf_ref, vb_ref)
    o_refs = (of_ref, ob_ref)
    beta_refs = (beta_f_ref, beta_b_ref)
    ld_refs = (ld_f_ref, ld_b_ref)

    def to_col(x_row):
        return jnp.sum(jnp.where(diag, x_row, 0.0), axis=1, keepdims=True)

    units = {}
    for d in range(2):
        for ci in range(nch):
            rows = slice(ci * c, (ci + 1) * c)
            q = q_refs[d][rows, :]
            k = k_refs[d][rows, :]
            qb16 = q.astype(BF16)
            kb16 = k.astype(BF16)
            kk = _dot_nt(kb16, kb16)
            qk = _dot_nt(qb16, kb16)
            for r in range(rep):
                beta_row = beta_refs[d][r, ci:ci + 1, :]
                g_row = ld_refs[d][r, ci:ci + 1, :]
                g_col = to_col(g_row)
                beta_col = to_col(beta_row)
                gc_col = jnp.sum(jnp.where(incl[d], g_row, 0.0), axis=1, keepdims=True)
                gc_row = jnp.sum(jnp.where(incl[1 - d], g_col, 0.0), axis=0, keepdims=True)
                dec = jnp.exp(jnp.minimum(gc_col - gc_row, 0.0))
                gc_end = gc_col[end[d]:end[d] + 1, :]
                eg = jnp.exp(gc_col)
                v = v_refs[d][rows, r * HEAD_DIM:(r + 1) * HEAD_DIM]
                units[d, ci, r] = dict(
                    m=jnp.where(strict[d], -(beta_col * kk * dec), 0.0),
                    rhs_v=(beta_col * v).astype(BF16),
                    rhs_k=(beta_col * eg * k).astype(BF16),
                    a=jnp.where(incl[d], qk * dec, 0.0).astype(BF16),
                    kd_t=(k * jnp.exp(gc_end - gc_col)).T.astype(BF16),
                    eg=eg, dec_end=jnp.exp(gc_end), qb16=qb16)

    keys = list(units)
    for key, tinv in zip(keys, _neumann_inverses([units[key]["m"] for key in keys], eye)):
        u = units[key]
        sol = jnp.dot(tinv.astype(BF16), jnp.concatenate([u["rhs_v"], u["rhs_k"]], axis=1),
                      preferred_element_type=F32)
        u["sol_v"] = sol[:, :HEAD_DIM]
        u["kq"] = jnp.concatenate([sol[:, HEAD_DIM:].astype(BF16), u["qb16"]], axis=0)
        u["a_kd"] = jnp.concatenate([u["a"], u["kd_t"]], axis=0)

    chains = [(d, r) for d in range(2) for r in range(rep)]
    states = {ch: st_ref[ch[0], ch[1]] for ch in chains}
    for step in range(nch):
        for d, r in chains:
            ci = step if d == 0 else nch - 1 - step
            u = units[d, ci, r]
            st = states[d, r]
            ks_qs = jnp.dot(u["kq"], st.astype(BF16), preferred_element_type=F32)
            w16 = (u["sol_v"] - ks_qs[:c]).astype(BF16)
            aw_kw = jnp.dot(u["a_kd"], w16, preferred_element_type=F32)
            o_refs[d][ci * c:(ci + 1) * c, r * HEAD_DIM:(r + 1) * HEAD_DIM] = u["eg"] * ks_qs[c:] + aw_kw[:c]
            states[d, r] = st * u["dec_end"] + aw_kw[c:]
    for d, r in chains:
        st_ref[d, r] = states[d, r]


def _dn_scan_call(qkv, beta, ld, nb, t, gt, key_w, rep):
    ng = t // gt
    hd = HEAD_DIM
    hq = key_w // hd
    nch = gt // CHUNK
    koff = key_w // hd
    voff = 2 * key_w // (rep * hd)

    def spec(width, off, bwd):
        if bwd:
            return pl.BlockSpec((None, gt, width), lambda b, h, s: (b, _bwd_group(s, ng), off + h))
        return pl.BlockSpec((None, gt, width), lambda b, h, s: (b, s, off + h))

    def sspec(d):
        if d == 1:
            return pl.BlockSpec((None, None, rep, None, nch, CHUNK),
                                lambda b, h, s: (1, b, h, _bwd_group(s, ng), 0, 0))
        return pl.BlockSpec((None, None, rep, None, nch, CHUNK), lambda b, h, s: (0, b, h, s, 0, 0))

    in_specs, args = [], []
    for d in range(2):
        in_specs += [spec(hd, 0, d == 1), spec(hd, koff, d == 1), spec(rep * hd, voff, d == 1)]
        args += [qkv, qkv, qkv]
    in_specs += [sspec(0), sspec(0), sspec(1), sspec(1)]
    args += [beta, ld, beta, ld]
    out = jax.ShapeDtypeStruct((nb, t, hq * rep * hd), F32)
    kern = functools.partial(_dn_scan_kernel, gt=gt, rep=rep)
    return pl.pallas_call(
        kern,
        out_shape=(out, out),
        grid=(nb, hq, ng),
        in_specs=in_specs,
        out_specs=(pl.BlockSpec((None, gt, rep * hd), lambda b, h, s: (b, s, h)),
                   pl.BlockSpec((None, gt, rep * hd), lambda b, h, s: (b, _bwd_group(s, ng), h))),
        scratch_shapes=[pltpu.VMEM((2, rep, hd, hd), F32)],
        compiler_params=_params(("parallel", "parallel", "arbitrary")),
        name="deltanet_scan",
    )(*args)


def _final_norm_kernel(h_ref, w_ref, o_ref):
    x = h_ref[...]
    o_ref[...] = x * lax.rsqrt(jnp.mean(x * x, axis=-1, keepdims=True) + EPS) * w_ref[...]


def _final_norm_call(h3, w, ctx, seq):
    nb, t, d = h3.shape
    skip = ctx // ctx
    return pl.pallas_call(
        _final_norm_kernel,
        out_shape=jax.ShapeDtypeStruct((nb, seq, d), F32),
        grid=(nb, seq // ctx),
        in_specs=[pl.BlockSpec((None, ctx, d), lambda b, i: (b, i + skip, 0)),
                  pl.BlockSpec((1, d), lambda b, i: (0, 0))],
        out_specs=pl.BlockSpec((None, ctx, d), lambda b, i: (b, i, 0)),
        compiler_params=_params(("parallel", "parallel")),
        name="final_norm",
    )(h3, w.reshape(1, d))


def _scan_scalars(x, nb, t, gt):
    hv = x.shape[-1]
    x = x.reshape(2, nb, t, hv).transpose(0, 1, 3, 2)
    return x.reshape(2, nb, hv, t // gt, gt // CHUNK, CHUNK)


def kernel(x, c, ctx, c_ctx, w_ada, b_ada, norm_mix, norm_ffn, hg_lb_logits, hg_w_in, hg_w_f, hg_o_norm, hg_w_o, gla_w_in, gla_w_g1, gla_w_g2, gla_b_g, gla_o_norm, gla_w_o, dn_w_in, dn_conv, dn_w_b, dn_w_a, dn_a_log, dn_dt_bias, dn_o_norm, dn_w_o, ffn_w_gate, ffn_w_up, ffn_w_down, final_norm):
    nb, seq, d = x.shape
    n_ctx = ctx.shape[1]
    depth = w_ada.shape[0]
    t = n_ctx + seq
    gt = n_ctx
    assert n_ctx % CHUNK == 0 and seq % gt == 0 and GRID_W == CHUNK and nb + 1 <= MOD_ROWS
    dims = (t, n_ctx, nb)
    m = nb * t

    (hg_w_in, hg_w_f, hg_w_o, gla_w_in, gla_w_g1, gla_w_g2, gla_w_o, dn_w_in, dn_w_b, dn_w_a, dn_w_o,
     ffn_w_gate, ffn_w_up, ffn_w_down) = (
        w.astype(BF16) for w in (hg_w_in, hg_w_f, hg_w_o, gla_w_in, gla_w_g1, gla_w_g2, gla_w_o, dn_w_in,
                                 dn_w_b, dn_w_a, dn_w_o, ffn_w_gate, ffn_w_up, ffn_w_down))

    p = jax.nn.softmax(hg_lb_logits.astype(F32), axis=0)
    lower_bounds = jnp.cumsum(p, axis=0) - p[0]

    cc = jnp.zeros((MOD_ROWS, d), F32).at[:nb].set(c).at[nb].set(c_ctx)
    mods = _ada_call(cc, w_ada, b_ada)

    h = jnp.concatenate([ctx, x], axis=1).reshape(m, d)

    hg_heads = d // HEAD_DIM
    hg_kw = hg_w_f.shape[-1]
    gla_kw = gla_w_g2.shape[-1]
    gla_dk = gla_kw // GLA_HEADS
    gla_dv = d // GLA_HEADS
    dn_key_w = d
    dn_hv = dn_w_b.shape[-1]
    dn_val_w = dn_hv * HEAD_DIM
    dn_qkv_w = 2 * dn_key_w + dn_val_w
    rep = dn_hv // (dn_key_w // HEAD_DIM)

    for i in range(depth):
        kind, j = i % 3, i // 3
        src = _NormSrc(h, norm_mix.reshape(depth, 1, d), mods, i, 0, 1, dims)
        if kind == 0:
            proj = _proj_call(lambda acc: _silu(acc) * HEAD_DIM ** -0.5, src, hg_w_in, j,
                              "hgrn2_in_proj", hg_kw)
            kk, lf = _hg_f_call(src, hg_w_f, j, lower_bounds[i])
            p3 = proj.reshape(nb, t, -1)
            kk4 = kk.reshape(2, nb, t, hg_kw)
            o_f, o_b = _gla_scan_call((p3, 0), ((kk4, 0, 0), (kk4, 1, 0)), (p3, hg_kw),
                                      lf.reshape(2, nb, t, hg_kw), nb, t, gt, hg_heads, HEAD_DIM,
                                      d // hg_heads)
            h = _gated_out_call(o_f.reshape(m, -1), o_b.reshape(m, -1), proj, hg_kw + d, hg_o_norm, hg_w_o, j,
                                h, mods, i, 2, dims, "hgrn2_out_proj")
        elif kind == 1:
            proj = _proj_call(lambda acc: acc * gla_dk ** -0.5, src, gla_w_in, j,
                              "gla_in_proj", gla_kw)
            lg = _gla_gate_call(src, gla_w_g1, gla_w_g2, gla_b_g, j)
            p3 = proj.reshape(nb, t, -1)
            o_f, o_b = _gla_scan_call((p3, 0), ((p3, None, gla_kw), (p3, None, gla_kw)), (p3, 2 * gla_kw),
                                      lg.reshape(2, nb, t, gla_kw), nb, t, gt, GLA_HEADS, gla_dk, gla_dv)
            h = _gated_out_call(o_f.reshape(m, -1), o_b.reshape(m, -1), proj, 2 * gla_kw + d, gla_o_norm,
                                gla_w_o, j, h, mods, i, 2, dims, "gla_out_proj")
        else:
            proj = _proj_call(None, src, dn_w_in, j, "deltanet_in_proj")
            beta, ld = _dn_bg_call(src, dn_w_b, dn_w_a, dn_a_log, dn_dt_bias, j)
            qkv = _dn_conv_call(proj.reshape(nb, t, -1), dn_conv, j, nb, t, gt, dn_qkv_w, dn_key_w)
            o_f, o_b = _dn_scan_call(qkv, _scan_scalars(beta, nb, t, gt), _scan_scalars(ld, nb, t, gt),
                                     nb, t, gt, dn_key_w, rep)
            h = _gated_out_call(o_f.reshape(m, -1), o_b.reshape(m, -1), proj, dn_qkv_w, dn_o_norm, dn_w_o, j,
                                h, mods, i, 2, dims, "deltanet_out_proj")
        src = _NormSrc(h, norm_ffn.reshape(depth, 1, d), mods, i, 3, 4, dims)
        u = _swiglu_call(src, ffn_w_gate, ffn_w_up, i)
        h = _residual_call(u, ffn_w_down, i, h, mods, i, 5, dims, "ffn_down_proj")

    return _final_norm_call(h.reshape(nb, t, d), final_norm, n_ctx, seq)
```

```python
import functools
from typing import NamedTuple

import jax
import jax.numpy as jnp
from jax import lax
from jax.experimental import pallas as pl
from jax.experimental.pallas import tpu as pltpu

F32 = jnp.float32
BF16 = jnp.bfloat16

EPS = 1e-6
GATE_CLIP = 30.0
CHUNK = 64
GRID_W = 64
CONV_K = 5
GLA_HEADS = 4
GLA_GATE_NORM = 16.0
HEAD_DIM = 128
EXP_RANGE = 80.0

V7X_VMEM_LIMIT = 56 * 1024 * 1024
VMEM_BLOCK_BUDGET = 44 * 1024 * 1024
BF16_SUBLANES = 16
MOD_ROWS = 8


def _params(sem):
    return pltpu.CompilerParams(dimension_semantics=sem, vmem_limit_bytes=V7X_VMEM_LIMIT)


def _sigmoid(x):
    return 1.0 / (1.0 + jnp.exp(-x))


def _silu(x):
    return x * _sigmoid(x)


def _softplus(x):
    return jnp.maximum(x, 0.0) + jnp.log1p(jnp.exp(-jnp.abs(x)))


def _log_sigmoid(x):
    return -_softplus(-x)


def _row_tile(t, target):
    best = 8
    for d in range(8, min(t, target) + 1, 8):
        if t % d == 0:
            best = d
    return best


def _col_tile(n, target):
    for c in (target, 512, 256, 128):
        if c <= target and n % c == 0:
            return c
    return n


def _mod_rows(m_ref, i, tm, tiles_per_batch, ctx, nb):
    b = i // tiles_per_batch
    r0 = (i % tiles_per_batch) * tm
    lat = m_ref[pl.ds(b, 1), :]
    cx = m_ref[nb:nb + 1, :]
    rows = lax.broadcasted_iota(jnp.int32, (tm, 1), 0) + r0
    return jnp.where(rows < ctx, cx, lat)


def _ada_kernel(c_ref, w_ref, b_ref, o_ref):
    s = _silu(c_ref[...]).astype(BF16)
    o_ref[...] = jnp.dot(s, w_ref[...].astype(BF16), preferred_element_type=F32) + b_ref[...]


def _ada_call(cc, w_ada, b_ada):
    depth, d, n = w_ada.shape
    tn = _col_tile(n, 1024)
    return pl.pallas_call(
        _ada_kernel,
        out_shape=jax.ShapeDtypeStruct((depth, MOD_ROWS, n), F32),
        grid=(depth, n // tn),
        in_specs=[pl.BlockSpec((MOD_ROWS, d), lambda i, j: (0, 0)),
                  pl.BlockSpec((None, d, tn), lambda i, j: (i, 0, j)),
                  pl.BlockSpec((None, 1, tn), lambda i, j: (i, 0, j))],
        out_specs=pl.BlockSpec((None, MOD_ROWS, tn), lambda i, j: (i, 0, j)),
        compiler_params=_params(("parallel", "parallel")),
        name="ada_mods",
    )(cc, w_ada, b_ada.reshape(depth, 1, n))


class _NormSrc(NamedTuple):
    h: jax.Array
    norm_w: jax.Array
    mods: jax.Array
    layer: int
    g_shift: int
    g_scale: int
    dims: tuple


def _dot(x, w_ref):
    return jnp.dot(x, w_ref[...].astype(BF16), preferred_element_type=F32)


def _row_chunk(tm):
    best = tm
    for rc in range(BF16_SUBLANES, min(tm, 272) + 1, BF16_SUBLANES):
        if tm % rc == 0:
            best = rc
    return best


def _fused(epilogue, src, tm, n_w):
    t, ctx, nb = src.dims
    tpb = t // tm
    rc = _row_chunk(tm)

    def kern(h_ref, nw_ref, sh_ref, sc_ref, *rest):
        a_ref = rest[-1]
        w_refs = rest[:n_w]
        n_out = epilogue.n_out
        par_refs = rest[n_w:len(rest) - 1 - n_out]
        out_refs = rest[len(rest) - 1 - n_out:-1]
        i = pl.program_id(0)
        b = i // tpb
        r0 = (i % tpb) * tm

        def compute(first):
            ws = [w[...].astype(BF16) for w in w_refs]
            for r in range(tm // rc):
                rows = slice(r * rc, (r + 1) * rc)
                if first:
                    x = h_ref[rows, :]
                    y = x * lax.rsqrt(jnp.mean(x * x, axis=-1, keepdims=True) + EPS) * nw_ref[...]
                    is_ctx = lax.broadcasted_iota(jnp.int32, (rc, 1), 0) + (r0 + r * rc) < ctx
                    scale = jnp.where(is_ctx, sc_ref[nb:nb + 1, :], sc_ref[pl.ds(b, 1), :])
                    shift = jnp.where(is_ctx, sh_ref[nb:nb + 1, :], sh_ref[pl.ds(b, 1), :])
                    a = (y * (1.0 + scale) + shift).astype(BF16)
                    a_ref[rows, :] = a
                else:
                    a = a_ref[rows, :]
                accs = [jnp.dot(a, w, preferred_element_type=F32) for w in ws]
                outs = epilogue(accs, par_refs)
                for o_ref, o in zip(out_refs, outs):
                    o_ref[rows, :] = o.astype(o_ref.dtype)

        @pl.when(pl.program_id(1) == 0)
        def _():
            compute(True)

        @pl.when(pl.program_id(1) > 0)
        def _():
            compute(False)

    return kern


def _src_specs(src, tm):
    d = src.h.shape[1]
    specs = [pl.BlockSpec((tm, d), lambda i, j: (i, 0), pipeline_mode=pl.Buffered(1)),
             pl.BlockSpec((None, 1, d), lambda i, j: (src.layer, 0, 0)),
             pl.BlockSpec((None, MOD_ROWS, d), lambda i, j: (src.layer, 0, src.g_shift)),
             pl.BlockSpec((None, MOD_ROWS, d), lambda i, j: (src.layer, 0, src.g_scale))]
    return specs, [src.h, src.norm_w, src.mods, src.mods]


class _Epilogue:
    def __init__(self, fn, n_out=1):
        self.fn = fn
        self.n_out = n_out

    def __call__(self, accs, par_refs):
        outs = self.fn(accs, par_refs)
        return outs if isinstance(outs, (tuple, list)) else (outs,)


def _fused_call(epilogue, src, tm, grid_inner, n_w, in_specs, args, out_shape, out_specs, name):
    m, d = src.h.shape
    specs, src_args = _src_specs(src, tm)
    return pl.pallas_call(
        _fused(epilogue, src, tm, n_w),
        out_shape=out_shape,
        grid=(m // tm, grid_inner),
        in_specs=specs + in_specs,
        out_specs=out_specs,
        scratch_shapes=[pltpu.VMEM((tm, d), BF16)],
        compiler_params=_params(("parallel", "arbitrary")),
        name=name,
    )(*src_args, *args)


def _mm_tiles(t, k, n, n_w, out_bytes, x_bytes=4, fixed=0, tm_target=1088, tn_target=512):
    tm = _row_tile(t, tm_target)
    while tm > 64:
        for tn in sorted({tn_target, 512, 256}, reverse=True):
            if tn > tn_target:
                continue
            need = tm * k * x_bytes + fixed + 2 * n_w * k * tn * 2 + 2 * tm * tn * out_bytes
            if n % tn == 0 and need <= VMEM_BLOCK_BUDGET:
                return tm, tn
        tm = _row_tile(t, tm - 8)
    return tm, _col_tile(n, 128)


def _fused_tiles(t, k, n, n_w, out_bytes, tn_target=1024):
    return _mm_tiles(t, k, n, n_w, out_bytes, x_bytes=6, fixed=8 * 1024 * 1024, tn_target=tn_target)


def _proj_call(epi_q, src, w, layer_idx, name, q_cols=0):
    m, k = src.h.shape
    n = w.shape[-1]
    tm, tn = _fused_tiles(src.dims[0], k, n, 1, 4)
    while q_cols % tn:
        tn //= 2

    def epi(accs, par_refs):
        acc = accs[0]
        if epi_q is None:
            return acc
        return jnp.where(pl.program_id(1) * tn < q_cols, epi_q(acc), acc)

    return _fused_call(
        _Epilogue(epi), src, tm, n // tn, 1,
        [pl.BlockSpec((None, k, tn), lambda i, j: (layer_idx, 0, j))], [w],
        jax.ShapeDtypeStruct((m, n), F32), pl.BlockSpec((tm, tn), lambda i, j: (i, j)), name)


def _hg_f_epilogue(accs, par_refs):
    z = jnp.clip(accs[0], -GATE_CLIP, GATE_CLIP)
    lb = par_refs[0][...]
    u = jnp.exp(-z)
    r = 1.0 / (1.0 + u)
    return (1.0 - lb) * (u * r), jnp.log((1.0 + lb * u) * r)


def _hg_f_call(src, w_f, layer_idx, lb):
    m, k = src.h.shape
    n = w_f.shape[-1]
    tm, tn = _fused_tiles(src.dims[0], k, n, 1, 8)
    nt = n // tn
    out = jax.ShapeDtypeStruct((2, m, n), F32)
    ospec = pl.BlockSpec((None, tm, tn), lambda i, j: (j // nt, i, j % nt))
    return _fused_call(
        _Epilogue(_hg_f_epilogue, 2), src, tm, 2 * nt, 1,
        [pl.BlockSpec((None, None, k, tn), lambda i, j: (layer_idx, j // nt, 0, j % nt)),
         pl.BlockSpec((1, tn), lambda i, j: (0, j % nt))],
        [w_f, lb.reshape(1, n)], (out, out), (ospec, ospec), "hgrn2_gates")


def _gla_gate_epilogue(accs, par_refs):
    w2_ref, b_ref = par_refs
    gl = jnp.dot(accs[0].astype(BF16), w2_ref[...].astype(BF16), preferred_element_type=F32) + b_ref[...]
    return _log_sigmoid(gl) * (1.0 / GLA_GATE_NORM)


def _gla_gate_call(src, w_g1, w_g2, b_g, layer_idx):
    m, k = src.h.shape
    rank, kw = w_g2.shape[-2:]
    tm = _row_tile(src.dims[0], 544)
    return _fused_call(
        _Epilogue(_gla_gate_epilogue), src, tm, 2, 1,
        [pl.BlockSpec((None, None, k, rank), lambda i, d: (layer_idx, d, 0, 0)),
         pl.BlockSpec((None, None, rank, kw), lambda i, d: (layer_idx, d, 0, 0)),
         pl.BlockSpec((None, None, 1, kw), lambda i, d: (layer_idx, d, 0, 0))],
        [w_g1, w_g2, b_g.reshape(b_g.shape[0], 2, 1, kw)],
        jax.ShapeDtypeStruct((2, m, kw), F32), pl.BlockSpec((None, tm, kw), lambda i, d: (d, i, 0)),
        "gla_gates")


def _dn_bg_epilogue(accs, par_refs):
    alog_ref, dtb_ref = par_refs
    beta = _sigmoid(accs[0])
    dt = _softplus(accs[1] + dtb_ref[...])
    return beta, -jnp.exp(alog_ref[...]) * dt


def _dn_bg_call(src, w_b, w_a, a_log, dt_bias, layer_idx):
    m, k = src.h.shape
    hv = w_b.shape[-1]
    tm = _row_tile(src.dims[0], 544)
    out = jax.ShapeDtypeStruct((2, m, hv), F32)
    wspec = pl.BlockSpec((None, None, k, hv), lambda i, d: (layer_idx, d, 0, 0))
    pspec = pl.BlockSpec((None, None, 1, hv), lambda i, d: (layer_idx, d, 0, 0))
    ospec = pl.BlockSpec((None, tm, hv), lambda i, d: (d, i, 0))
    nl = a_log.shape[0]
    return _fused_call(
        _Epilogue(_dn_bg_epilogue, 2), src, tm, 2, 2, [wspec, wspec, pspec, pspec],
        [w_b, w_a, a_log.reshape(nl, 2, 1, hv), dt_bias.reshape(nl, 2, 1, hv)],
        (out, out), (ospec, ospec), "deltanet_beta_decay")


def _swiglu_epilogue(accs, par_refs):
    return _silu(accs[0]) * accs[1]


def _swiglu_call(src, w_gate, w_up, layer):
    m, k = src.h.shape
    n = w_gate.shape[-1]
    tm, tn = _fused_tiles(src.dims[0], k, n, 2, 2, tn_target=512)
    wspec = pl.BlockSpec((None, k, tn), lambda i, j: (layer, 0, j))
    return _fused_call(
        _Epilogue(_swiglu_epilogue), src, tm, n // tn, 2, [wspec, wspec], [w_gate, w_up],
        jax.ShapeDtypeStruct((m, n), BF16), pl.BlockSpec((tm, tn), lambda i, j: (i, j)), "ffn_swiglu")


def _residual_kernel(x_ref, w_ref, h_ref, g_ref, o_ref, *, tm, tpb, ctx, nb):
    i = pl.program_id(0)
    b = i // tpb
    r0 = (i % tpb) * tm
    rc = _row_chunk(tm)
    w = w_ref[...].astype(BF16)
    for r in range(tm // rc):
        rows = slice(r * rc, (r + 1) * rc)
        is_ctx = lax.broadcasted_iota(jnp.int32, (rc, 1), 0) + (r0 + r * rc) < ctx
        gate = jnp.where(is_ctx, g_ref[nb:nb + 1, :], g_ref[pl.ds(b, 1), :])
        o_ref[rows, :] = h_ref[rows, :] + gate * jnp.dot(x_ref[rows, :], w, preferred_element_type=F32)


def _residual_call(y, w, w_idx, h, mods, layer, g_gate, dims, name):
    m, k = y.shape
    d = h.shape[1]
    t, ctx, nb = dims
    tm, tn = _mm_tiles(t, k, d, 1, 8)
    nt = d // tn
    kern = functools.partial(_residual_kernel, tm=tm, tpb=t // tm, ctx=ctx, nb=nb)
    return pl.pallas_call(
        kern,
        out_shape=jax.ShapeDtypeStruct(h.shape, F32),
        grid=(m // tm, nt),
        in_specs=[pl.BlockSpec((tm, k), lambda i, j: (i, 0)),
                  pl.BlockSpec((None, k, tn), lambda i, j: (w_idx, 0, j)),
                  pl.BlockSpec((tm, tn), lambda i, j: (i, j)),
                  pl.BlockSpec((None, MOD_ROWS, tn), lambda i, j: (layer, 0, g_gate * nt + j))],
        out_specs=pl.BlockSpec((tm, tn), lambda i, j: (i, j)),
        input_output_aliases={2: 0},
        compiler_params=_params(("parallel", "arbitrary")),
        name=name,
    )(y, w, h, mods)


def _gated_out_kernel(of_ref, ob_ref, g_ref, nw_ref, w_ref, h_ref, m_ref, o_ref, *, dv, tm, rc, tpb, ctx, nb):
    i = pl.program_id(0)
    b = i // tpb
    r0 = (i % tpb) * tm
    nw = nw_ref[...]
    w = w_ref[...].astype(BF16)
    for r in range(tm // rc):
        rows = slice(r * rc, (r + 1) * rc)
        ys = []
        for hh in range(of_ref.shape[1] // dv):
            cols = slice(hh * dv, (hh + 1) * dv)
            o = of_ref[rows, cols] + ob_ref[rows, cols]
            y = o * lax.rsqrt(jnp.mean(o * o, axis=-1, keepdims=True) + EPS) * nw
            ys.append((y * _silu(g_ref[rows, cols])).astype(BF16))
        y = ys[0] if len(ys) == 1 else jnp.concatenate(ys, axis=1)
        is_ctx = lax.broadcasted_iota(jnp.int32, (rc, 1), 0) + (r0 + r * rc) < ctx
        gate = jnp.where(is_ctx, m_ref[nb:nb + 1, :], m_ref[pl.ds(b, 1), :])
        o_ref[rows, :] = h_ref[rows, :] + gate * jnp.dot(y, w, preferred_element_type=F32)


def _gated_out_call(o_f, o_b, gate_arr, gate_col0, o_norm, w_o, w_idx, h, mods, layer, g_gate, dims, name):
    m, width = o_f.shape
    d = h.shape[1]
    t, ctx, nb = dims
    dv = o_norm.shape[-1]
    tm = _row_tile(t, 272)
    while tm > 8 and 3 * 2 * tm * width * 4 + width * d * 2 + 4 * tm * d * 4 > 40 * 1024 * 1024:
        tm = _row_tile(t, tm - 8)
    rc = _row_tile(tm, 136)
    kern = functools.partial(_gated_out_kernel, dv=dv, tm=tm, rc=rc, tpb=t // tm, ctx=ctx, nb=nb)
    row_spec = pl.BlockSpec((tm, width), lambda i: (i, 0))
    return pl.pallas_call(
        kern,
        out_shape=jax.ShapeDtypeStruct(h.shape, F32),
        grid=(m // tm,),
        in_specs=[row_spec, row_spec,
                  pl.BlockSpec((tm, width), lambda i: (i, gate_col0 // width)),
                  pl.BlockSpec((None, 1, dv), lambda i: (w_idx, 0, 0)),
                  pl.BlockSpec((None, width, d), lambda i: (w_idx, 0, 0), pipeline_mode=pl.Buffered(1)),
                  pl.BlockSpec((tm, d), lambda i: (i, 0)),
                  pl.BlockSpec((None, MOD_ROWS, d), lambda i: (layer, 0, g_gate))],
        out_specs=pl.BlockSpec((tm, d), lambda i: (i, 0)),
        input_output_aliases={5: 0},
        compiler_params=_params(("parallel",)),
        name=name,
    )(o_f, o_b, gate_arr, o_norm.reshape(o_norm.shape[0], 1, dv), w_o, h, mods)


def _chunk_cumsum(g, c, reverse):
    n = g.shape[0]
    pos = jnp.bitwise_and(lax.broadcasted_iota(jnp.int32, (n, 1), 0), c - 1)
    s = 1
    while s < c:
        if reverse:
            g = g + jnp.where(pos < c - s, pltpu.roll(g, n - s, 0), 0.0)
        else:
            g = g + jnp.where(pos >= s, pltpu.roll(g, s, 0), 0.0)
        s *= 2
    return g


def _bwd_group(s, ng):
    return jnp.where(s == 0, 0, ng - s)


def _dot_nt(a, b):
    return lax.dot_general(a, b, (((1,), (1,)), ((), ())), preferred_element_type=F32)


def _gla_scan_kernel(qf_ref, kf_ref, vf_ref, gf_ref, qb_ref, kb_ref, vb_ref, gb_ref,
                     of_ref, ob_ref, st_ref, b_ref, *, gt, dk, dv):
    c = CHUNK
    nch = gt // c

    @pl.when(pl.program_id(2) == 0)
    def _():
        st_ref[...] = jnp.zeros_like(st_ref)

    crow = lax.broadcasted_iota(jnp.int32, (c, c), 0)
    ccol = lax.broadcasted_iota(jnp.int32, (c, c), 1)
    masks = (ccol <= crow, ccol >= crow)
    mid = (c // 2 - 1, c // 2)
    end = (c - 1, 0)
    g_refs = (gf_ref, gb_ref)
    q_refs = (qf_ref, qb_ref)
    k_refs = (kf_ref, kb_ref)
    v_refs = (vf_ref, vb_ref)
    o_refs = (of_ref, ob_ref)

    worst = jnp.zeros((1, dk), F32)
    mag = jnp.ones((1, dk), F32)
    for d in range(2):
        mag = jnp.maximum(mag, jnp.max(jnp.maximum(jnp.abs(q_refs[d][...]), jnp.abs(k_refs[d][...])),
                                       axis=0, keepdims=True))
        b = _chunk_cumsum(g_refs[d][...], c, reverse=d == 1)
        b_ref[d] = b
        for ci in range(nch):
            r = b[ci * c + mid[d]:ci * c + mid[d] + 1, :]
            e = b[ci * c + end[d]:ci * c + end[d] + 1, :]
            worst = jnp.maximum(worst, jnp.maximum(-r, r - e))
    safe = jnp.max(worst + jnp.log(mag)) <= EXP_RANGE

    def scores_fast(d, ci, q, k, b):
        r = b[mid[d]:mid[d] + 1, :]
        qs = (q * jnp.exp(b - r)).astype(BF16)
        ks = (k * jnp.exp(r - b)).astype(BF16)
        return jnp.where(masks[d], _dot_nt(qs, ks), 0.0)

    def scores_exact(d, ci, q, k, b):
        lane = lax.broadcasted_iota(jnp.int32, (c, c), 1)

        def body(s, acc):
            ks = k_refs[d][pl.ds(ci * c + s, 1), :]
            bs = b_ref[d, pl.ds(ci * c + s, 1), :]
            term = q * ks * jnp.exp(jnp.minimum(b - bs, 0.0))
            return jnp.where(lane == s, jnp.sum(term, axis=1, keepdims=True), acc)

        acc = lax.fori_loop(0, c, body, jnp.zeros((c, c), F32))
        return jnp.where(masks[d], acc, 0.0)

    def run(scores):
        pre = {}
        for d in range(2):
            for ci in range(nch):
                rows = slice(ci * c, (ci + 1) * c)
                q = q_refs[d][rows, :]
                k = k_refs[d][rows, :]
                v = v_refs[d][rows, :]
                b = b_ref[d, rows, :]
                b_end = b[end[d]:end[d] + 1, :]
                a = scores(d, ci, q, k, b).astype(BF16)
                ke = (k * jnp.exp(b_end - b)).astype(BF16)
                pre[d, ci] = dict(
                    av=jnp.dot(a, v.astype(BF16), preferred_element_type=F32),
                    qe=(q * jnp.exp(b)).astype(BF16),
                    upd=jnp.dot(v.T.astype(BF16), ke, preferred_element_type=F32),
                    dec=jnp.exp(b_end))
        for d in range(2):
            st = st_ref[d]
            for step in range(nch):
                ci = step if d == 0 else nch - 1 - step
                p = pre[d, ci]
                o_refs[d][ci * c:(ci + 1) * c, :] = p["av"] + _dot_nt(p["qe"], st.astype(BF16))
                st = st * p["dec"] + p["upd"]
            st_ref[d] = st

    @pl.when(safe)
    def _():
        run(scores_fast)

    @pl.when(jnp.logical_not(safe))
    def _():
        run(scores_exact)


def _gla_scan_call(q_src, k_srcs, v_src, g_arr, nb, t, gt, heads, dk, dv):
    ng = t // gt

    def spec3(col0, width, bwd):
        off = col0 // width
        if bwd:
            return pl.BlockSpec((None, gt, width), lambda b, h, s: (b, _bwd_group(s, ng), off + h))
        return pl.BlockSpec((None, gt, width), lambda b, h, s: (b, s, off + h))

    def spec4(lead, col0, width, bwd):
        off = col0 // width
        if bwd:
            return pl.BlockSpec((None, None, gt, width),
                                lambda b, h, s: (lead, b, _bwd_group(s, ng), off + h))
        return pl.BlockSpec((None, None, gt, width), lambda b, h, s: (lead, b, s, off + h))

    def kspec(src, bwd):
        arr, lead, col0 = src
        return spec3(col0, dk, bwd) if lead is None else spec4(lead, col0, dk, bwd)

    in_specs, args = [], []
    for d in range(2):
        bwd = d == 1
        in_specs += [spec3(q_src[1], dk, bwd), kspec(k_srcs[d], bwd), spec3(v_src[1], dv, bwd),
                     spec4(d, 0, dk, bwd)]
        args += [q_src[0], k_srcs[d][0], v_src[0], g_arr]
    out = jax.ShapeDtypeStruct((nb, t, heads * dv), F32)
    kern = functools.partial(_gla_scan_kernel, gt=gt, dk=dk, dv=dv)
    return pl.pallas_call(
        kern,
        out_shape=(out, out),
        grid=(nb, heads, ng),
        in_specs=in_specs,
        out_specs=(pl.BlockSpec((None, gt, dv), lambda b, h, s: (b, s, h)),
                   pl.BlockSpec((None, gt, dv), lambda b, h, s: (b, _bwd_group(s, ng), h))),
        scratch_shapes=[pltpu.VMEM((2, dv, dk), F32), pltpu.VMEM((2, gt, dk), F32)],
        compiler_params=_params(("parallel", "parallel", "arbitrary")),
        name="gla_scan",
    )(*args)


def _dn_conv_kernel(x_ref, w_ref, o_ref, *, gt, n_norm_tiles, n_q_tiles, qscale):
    s = pl.program_id(1)
    j = pl.program_id(2)
    x = x_ref[...]
    w = w_ref[...]
    seg = jnp.where(s == 0, gt, GRID_W)
    r = lax.broadcasted_iota(jnp.int32, (gt, 1), 0)
    pos = jnp.where(s == 0, r, jnp.bitwise_and(r, GRID_W - 1))
    acc = x * w[CONV_K // 2:CONV_K // 2 + 1, :]
    for tap in range(CONV_K):
        off = tap - CONV_K // 2
        if off == 0:
            continue
        shifted = pltpu.roll(x, (-off) % gt, 0)
        ok = (pos + off >= 0) & (pos + off < seg)
        acc = acc + jnp.where(ok, shifted, 0.0) * w[tap:tap + 1, :]
    y = _silu(acc)
    o_ref[...] = y

    @pl.when(j < n_norm_tiles)
    def _():
        scale = jnp.where(j < n_q_tiles, qscale, 1.0)
        for hh in range(y.shape[1] // HEAD_DIM):
            cols = slice(hh * HEAD_DIM, (hh + 1) * HEAD_DIM)
            yh = y[:, cols]
            o_ref[:, cols] = yh * (lax.rsqrt(jnp.sum(yh * yh, axis=-1, keepdims=True) + EPS) * scale)


def _dn_conv_call(p, conv_w, layer_idx, nb, t, gt, qkv_w, key_w):
    tc = min(512, key_w)
    kern = functools.partial(_dn_conv_kernel, gt=gt, n_norm_tiles=2 * key_w // tc,
                             n_q_tiles=key_w // tc, qscale=HEAD_DIM ** -0.5)
    return pl.pallas_call(
        kern,
        out_shape=jax.ShapeDtypeStruct((nb, t, qkv_w), F32),
        grid=(nb, t // gt, qkv_w // tc),
        in_specs=[pl.BlockSpec((None, gt, tc), lambda b, s, j: (b, s, j)),
                  pl.BlockSpec((None, CONV_K, tc), lambda b, s, j: (layer_idx, 0, j))],
        out_specs=pl.BlockSpec((None, gt, tc), lambda b, s, j: (b, s, j)),
        compiler_params=_params(("parallel", "parallel", "parallel")),
        name="deltanet_conv",
    )(p, conv_w)


def _mm3(a, b):
    a1 = a.astype(BF16)
    a2 = (a - a1.astype(F32)).astype(BF16)
    b1 = b.astype(BF16)
    b2 = (b - b1.astype(F32)).astype(BF16)
    return (jnp.dot(a1, b1, preferred_element_type=F32) + jnp.dot(a1, b2, preferred_element_type=F32)
            + jnp.dot(a2, b1, preferred_element_type=F32))


def _mm1(a, b):
    return jnp.dot(a.astype(BF16), b.astype(BF16), preferred_element_type=F32)


INVERSE_EXACT_LEVELS = 0


def _neumann_inverses(ms, eye):
    c = CHUNK
    ps = [eye + m for m in ms]
    xs = [_mm3(m, m) for m in ms]
    levels = c.bit_length() - 2
    for lvl in range(levels):
        mm = _mm3 if lvl < INVERSE_EXACT_LEVELS else _mm1
        if lvl == levels - 1:
            ps = [p + mm(p, x) for p, x in zip(ps, xs)]
        else:
            rs = [mm(jnp.concatenate([x, p], axis=0), x) for p, x in zip(ps, xs)]
            xs = [r[:c] for r in rs]
            ps = [p + r[c:] for p, r in zip(ps, rs)]
    return ps


def _dn_scan_kernel(qf_ref, kf_ref, vf_ref, qb_ref, kb_ref, vb_ref, beta_f_ref, ld_f_ref,
                    beta_b_ref, ld_b_ref, of_ref, ob_ref, st_ref, *, gt, rep):
    c = CHUNK
    nch = gt // c

    @pl.when(pl.program_id(2) == 0)
    def _():
        st_ref[...] = jnp.zeros_like(st_ref)

    row = lax.broadcasted_iota(jnp.int32, (c, c), 0)
    col = lax.broadcasted_iota(jnp.int32, (c, c), 1)
    diag = row == col
    eye = jnp.where(diag, 1.0, 0.0)
    incl = (col <= row, col >= row)
    strict = (col < row, col > row)
    end = (c - 1, 0)
    q_refs = (qf_ref, qb_ref)
    k_refs = (kf_ref, kb_ref)
    v_refs = (vf_ref, vb_ref)
    o_refs = (of_ref, ob_ref)
    beta_refs = (beta_f_ref, beta_b_ref)
    ld_refs = (ld_f_ref, ld_b_ref)

    def to_col(x_row):
        return jnp.sum(jnp.where(diag, x_row, 0.0), axis=1, keepdims=True)

    units = {}
    for d in range(2):
        for ci in range(nch):
            rows = slice(ci * c, (ci + 1) * c)
            q = q_refs[d][rows, :]
            k = k_refs[d][rows, :]
            qb16 = q.astype(BF16)
            kb16 = k.astype(BF16)
            kk = _dot_nt(kb16, kb16)
            qk = _dot_nt(qb16, kb16)
            for r in range(rep):
                beta_row = beta_refs[d][r, ci:ci + 1, :]
                g_row = ld_refs[d][r, ci:ci + 1, :]
                g_col = to_col(g_row)
                beta_col = to_col(beta_row)
                gc_col = jnp.sum(jnp.where(incl[d], g_row, 0.0), axis=1, keepdims=True)
                gc_row = jnp.sum(jnp.where(incl[1 - d], g_col, 0.0), axis=0, keepdims=True)
                dec = jnp.exp(jnp.minimum(gc_col - gc_row, 0.0))
                gc_end = gc_col[end[d]:end[d] + 1, :]
                eg = jnp.exp(gc_col)
                v = v_refs[d][rows, r * HEAD_DIM:(r + 1) * HEAD_DIM]
                units[d, ci, r] = dict(
                    m=jnp.where(strict[d], -(beta_col * kk * dec), 0.0),
                    rhs_v=(beta_col * v).astype(BF16),
                    rhs_k=(beta_col * eg * k).astype(BF16),
                    a=jnp.where(incl[d], qk * dec, 0.0).astype(BF16),
                    kd_t=(k * jnp.exp(gc_end - gc_col)).T.astype(BF16),
                    eg=eg, dec_end=jnp.exp(gc_end), qb16=qb16)

    keys = list(units)
    for key, tinv in zip(keys, _neumann_inverses([units[key]["m"] for key in keys], eye)):
        u = units[key]
        sol = jnp.dot(tinv.astype(BF16), jnp.concatenate([u["rhs_v"], u["rhs_k"]], axis=1),
                      preferred_element_type=F32)
        u["sol_v"] = sol[:, :HEAD_DIM]
        u["kq"] = jnp.concatenate([sol[:, HEAD_DIM:].astype(BF16), u["qb16"]], axis=0)
        u["a_kd"] = jnp.concatenate([u["a"], u["kd_t"]], axis=0)

    chains = [(d, r) for d in range(2) for r in range(rep)]
    states = {ch: st_ref[ch[0], ch[1]] for ch in chains}
    for step in range(nch):
        for d, r in chains:
            ci = step if d == 0 else nch - 1 - step
            u = units[d, ci, r]
            st = states[d, r]
            ks_qs = jnp.dot(u["kq"], st.astype(BF16), preferred_element_type=F32)
            w16 = (u["sol_v"] - ks_qs[:c]).astype(BF16)
            aw_kw = jnp.dot(u["a_kd"], w16, preferred_element_type=F32)
            o_refs[d][ci * c:(ci + 1) * c, r * HEAD_DIM:(r + 1) * HEAD_DIM] = u["eg"] * ks_qs[c:] + aw_kw[:c]
            states[d, r] = st * u["dec_end"] + aw_kw[c:]
    for d, r in chains:
        st_ref[d, r] = states[d, r]


def _dn_scan_call(qkv, beta, ld, nb, t, gt, key_w, rep):
    ng = t // gt
    hd = HEAD_DIM
    hq = key_w // hd
    nch = gt // CHUNK
    koff = key_w // hd
    voff = 2 * key_w // (rep * hd)

    def spec(width, off, bwd):
        if bwd:
            return pl.BlockSpec((None, gt, width), lambda b, h, s: (b, _bwd_group(s, ng), off + h))
        return pl.BlockSpec((None, gt, width), lambda b, h, s: (b, s, off + h))

    def sspec(d):
        if d == 1:
            return pl.BlockSpec((None, None, rep, None, nch, CHUNK),
                                lambda b, h, s: (1, b, h, _bwd_group(s, ng), 0, 0))
        return pl.BlockSpec((None, None, rep, None, nch, CHUNK), lambda b, h, s: (0, b, h, s, 0, 0))

    in_specs, args = [], []
    for d in range(2):
        in_specs += [spec(hd, 0, d == 1), spec(hd, koff, d == 1), spec(rep * hd, voff, d == 1)]
        args += [qkv, qkv, qkv]
    in_specs += [sspec(0), sspec(0), sspec(1), sspec(1)]
    args += [beta, ld, beta, ld]
    out = jax.ShapeDtypeStruct((nb, t, hq * rep * hd), F32)
    kern = functools.partial(_dn_scan_kernel, gt=gt, rep=rep)
    return pl.pallas_call(
        kern,
        out_shape=(out, out),
        grid=(nb, hq, ng),
        in_specs=in_specs,
        out_specs=(pl.BlockSpec((None, gt, rep * hd), lambda b, h, s: (b, s, h)),
                   pl.BlockSpec((None, gt, rep * hd), lambda b, h, s: (b, _bwd_group(s, ng), h))),
        scratch_shapes=[pltpu.VMEM((2, rep, hd, hd), F32)],
        compiler_params=_params(("parallel", "parallel", "arbitrary")),
        name="deltanet_scan",
    )(*args)


def _final_norm_kernel(h_ref, w_ref, o_ref):
    x = h_ref[...]
    o_ref[...] = x * lax.rsqrt(jnp.mean(x * x, axis=-1, keepdims=True) + EPS) * w_ref[...]


def _final_norm_call(h3, w, ctx, seq):
    nb, t, d = h3.shape
    skip = ctx // ctx
    return pl.pallas_call(
        _final_norm_kernel,
        out_shape=jax.ShapeDtypeStruct((nb, seq, d), F32),
        grid=(nb, seq // ctx),
        in_specs=[pl.BlockSpec((None, ctx, d), lambda b, i: (b, i + skip, 0)),
                  pl.BlockSpec((1, d), lambda b, i: (0, 0))],
        out_specs=pl.BlockSpec((None, ctx, d), lambda b, i: (b, i, 0)),
        compiler_params=_params(("parallel", "parallel")),
        name="final_norm",
    )(h3, w.reshape(1, d))


def _scan_scalars(x, nb, t, gt):
    hv = x.shape[-1]
    x = x.reshape(2, nb, t, hv).transpose(0, 1, 3, 2)
    return x.reshape(2, nb, hv, t // gt, gt // CHUNK, CHUNK)


def kernel(x, c, ctx, c_ctx, w_ada, b_ada, norm_mix, norm_ffn, hg_lb_logits, hg_w_in, hg_w_f, hg_o_norm, hg_w_o, gla_w_in, gla_w_g1, gla_w_g2, gla_b_g, gla_o_norm, gla_w_o, dn_w_in, dn_conv, dn_w_b, dn_w_a, dn_a_log, dn_dt_bias, dn_o_norm, dn_w_o, ffn_w_gate, ffn_w_up, ffn_w_down, final_norm):
    nb, seq, d = x.shape
    n_ctx = ctx.shape[1]
    depth = w_ada.shape[0]
    t = n_ctx + seq
    gt = n_ctx
    assert n_ctx % CHUNK == 0 and seq % gt == 0 and GRID_W == CHUNK and nb + 1 <= MOD_ROWS
    dims = (t, n_ctx, nb)
    m = nb * t

    (hg_w_in, hg_w_f, hg_w_o, gla_w_in, gla_w_g1, gla_w_g2, gla_w_o, dn_w_in, dn_w_b, dn_w_a, dn_w_o,
     ffn_w_gate, ffn_w_up, ffn_w_down) = (
        w.astype(BF16) for w in (hg_w_in, hg_w_f, hg_w_o, gla_w_in, gla_w_g1, gla_w_g2, gla_w_o, dn_w_in,
                                 dn_w_b, dn_w_a, dn_w_o, ffn_w_gate, ffn_w_up, ffn_w_down))

    p = jax.nn.softmax(hg_lb_logits.astype(F32), axis=0)
    lower_bounds = jnp.cumsum(p, axis=0) - p[0]

    cc = jnp.zeros((MOD_ROWS, d), F32).at[:nb].set(c).at[nb].set(c_ctx)
    mods = _ada_call(cc, w_ada, b_ada)

    h = jnp.concatenate([ctx, x], axis=1).reshape(m, d)

    hg_heads = d // HEAD_DIM
    hg_kw = hg_w_f.shape[-1]
    gla_kw = gla_w_g2.shape[-1]
    gla_dk = gla_kw // GLA_HEADS
    gla_dv = d // GLA_HEADS
    dn_key_w = d
    dn_hv = dn_w_b.shape[-1]
    dn_val_w = dn_hv * HEAD_DIM
    dn_qkv_w = 2 * dn_key_w + dn_val_w
    rep = dn_hv // (dn_key_w // HEAD_DIM)

    for i in range(depth):
        kind, j = i % 3, i // 3
        src = _NormSrc(h, norm_mix.reshape(depth, 1, d), mods, i, 0, 1, dims)
        if kind == 0:
            proj = _proj_call(lambda acc: _silu(acc) * HEAD_DIM ** -0.5, src, hg_w_in, j,
                              "hgrn2_in_proj", hg_kw)
            kk, lf = _hg_f_call(src, hg_w_f, j, lower_bounds[i])
            p3 = proj.reshape(nb, t, -1)
            kk4 = kk.reshape(2, nb, t, hg_kw)
            o_f, o_b = _gla_scan_call((p3, 0), ((kk4, 0, 0), (kk4, 1, 0)), (p3, hg_kw),
                                      lf.reshape(2, nb, t, hg_kw), nb, t, gt, hg_heads, HEAD_DIM,
                                      d // hg_heads)
            h = _gated_out_call(o_f.reshape(m, -1), o_b.reshape(m, -1), proj, hg_kw + d, hg_o_norm, hg_w_o, j,
                                h, mods, i, 2, dims, "hgrn2_out_proj")
        elif kind == 1:
            proj = _proj_call(lambda acc: acc * gla_dk ** -0.5, src, gla_w_in, j,
                              "gla_in_proj", gla_kw)
            lg = _gla_gate_call(src, gla_w_g1, gla_w_g2, gla_b_g, j)
            p3 = proj.reshape(nb, t, -1)
            o_f, o_b = _gla_scan_call((p3, 0), ((p3, None, gla_kw), (p3, None, gla_kw)), (p3, 2 * gla_kw),
                                      lg.reshape(2, nb, t, gla_kw), nb, t, gt, GLA_HEADS, gla_dk, gla_dv)
            h = _gated_out_call(o_f.reshape(m, -1), o_b.reshape(m, -1), proj, 2 * gla_kw + d, gla_o_norm,
                                gla_w_o, j, h, mods, i, 2, dims, "gla_out_proj")
        else:
            proj = _proj_call(None, src, dn_w_in, j, "deltanet_in_proj")
            beta, ld = _dn_bg_call(src, dn_w_b, dn_w_a, dn_a_log, dn_dt_bias, j)
            qkv = _dn_conv_call(proj.reshape(nb, t, -1), dn_conv, j, nb, t, gt, dn_qkv_w, dn_key_w)
            o_f, o_b = _dn_scan_call(qkv, _scan_scalars(beta, nb, t, gt), _scan_scalars(ld, nb, t, gt),
                                     nb, t, gt, dn_key_w, rep)
            h = _gated_out_call(o_f.reshape(m, -1), o_b.reshape(m, -1), proj, dn_qkv_w, dn_o_norm, dn_w_o, j,
                                h, mods, i, 2, dims, "deltanet_out_proj")
        src = _NormSrc(h, norm_ffn.reshape(depth, 1, d), mods, i, 3, 4, dims)
        u = _swiglu_call(src, ffn_w_gate, ffn_w_up, i)
        h = _residual_call(u, ffn_w_down, i, h, mods, i, 5, dims, "ffn_down_proj")

    return _final_norm_call(h.reshape(nb, t, d), final_norm, n_ctx, seq)
```

```python
import functools
from typing import NamedTuple

import jax
import jax.numpy as jnp
from jax import lax
from jax.experimental import pallas as pl
from jax.experimental.pallas import tpu as pltpu

F32 = jnp.float32
BF16 = jnp.bfloat16

EPS = 1e-6
GATE_CLIP = 30.0
CHUNK = 64
GRID_W = 64
CONV_K = 5
GLA_HEADS = 4
GLA_GATE_NORM = 16.0
HEAD_DIM = 128
EXP_RANGE = 80.0

V7X_VMEM_LIMIT = 56 * 1024 * 1024
VMEM_BLOCK_BUDGET = 44 * 1024 * 1024
BF16_SUBLANES = 16
MOD_ROWS = 8


def _params(sem):
    return pltpu.CompilerParams(dimension_semantics=sem, vmem_limit_bytes=V7X_VMEM_LIMIT)


def _sigmoid(x):
    return 1.0 / (1.0 + jnp.exp(-x))


def _silu(x):
    return x * _sigmoid(x)


def _softplus(x):
    return jnp.maximum(x, 0.0) + jnp.log1p(jnp.exp(-jnp.abs(x)))


def _log_sigmoid(x):
    return -_softplus(-x)


def _row_tile(t, target):
    best = 8
    for d in range(8, min(t, target) + 1, 8):
        if t % d == 0:
            best = d
    return best


def _col_tile(n, target):
    for c in (target, 512, 256, 128):
        if c <= target and n % c == 0:
            return c
    return n


def _mod_rows(m_ref, i, tm, tiles_per_batch, ctx, nb):
    b = i // tiles_per_batch
    r0 = (i % tiles_per_batch) * tm
    lat = m_ref[pl.ds(b, 1), :]
    cx = m_ref[nb:nb + 1, :]
    rows = lax.broadcasted_iota(jnp.int32, (tm, 1), 0) + r0
    return jnp.where(rows < ctx, cx, lat)


def _ada_kernel(c_ref, w_ref, b_ref, o_ref):
    s = _silu(c_ref[...]).astype(BF16)
    o_ref[...] = jnp.dot(s, w_ref[...].astype(BF16), preferred_element_type=F32) + b_ref[...]


def _ada_call(cc, w_ada, b_ada):
    depth, d, n = w_ada.shape
    tn = _col_tile(n, 1024)
    return pl.pallas_call(
        _ada_kernel,
        out_shape=jax.ShapeDtypeStruct((depth, MOD_ROWS, n), F32),
        grid=(depth, n // tn),
        in_specs=[pl.BlockSpec((MOD_ROWS, d), lambda i, j: (0, 0)),
                  pl.BlockSpec((None, d, tn), lambda i, j: (i, 0, j)),
                  pl.BlockSpec((None, 1, tn), lambda i, j: (i, 0, j))],
        out_specs=pl.BlockSpec((None, MOD_ROWS, tn), lambda i, j: (i, 0, j)),
        compiler_params=_params(("parallel", "parallel")),
        name="ada_mods",
    )(cc, w_ada, b_ada.reshape(depth, 1, n))


class _NormSrc(NamedTuple):
    h: jax.Array
    norm_w: jax.Array
    mods: jax.Array
    layer: int
    g_shift: int
    g_scale: int
    dims: tuple


def _dot(x, w_ref):
    return jnp.dot(x, w_ref[...].astype(BF16), preferred_element_type=F32)


def _row_chunk(tm):
    best = tm
    for rc in range(BF16_SUBLANES, min(tm, 272) + 1, BF16_SUBLANES):
        if tm % rc == 0:
            best = rc
    return best


def _fused(epilogue, src, tm, n_w):
    t, ctx, nb = src.dims
    tpb = t // tm
    rc = _row_chunk(tm)

    def kern(h_ref, nw_ref, sh_ref, sc_ref, *rest):
        a_ref = rest[-1]
        w_refs = rest[:n_w]
        n_out = epilogue.n_out
        par_refs = rest[n_w:len(rest) - 1 - n_out]
        out_refs = rest[len(rest) - 1 - n_out:-1]
        i = pl.program_id(0)
        b = i // tpb
        r0 = (i % tpb) * tm

        def compute(first):
            ws = [w[...].astype(BF16) for w in w_refs]
            for r in range(tm // rc):
                rows = slice(r * rc, (r + 1) * rc)
                if first:
                    x = h_ref[rows, :]
                    y = x * lax.rsqrt(jnp.mean(x * x, axis=-1, keepdims=True) + EPS) * nw_ref[...]
                    is_ctx = lax.broadcasted_iota(jnp.int32, (rc, 1), 0) + (r0 + r * rc) < ctx
                    scale = jnp.where(is_ctx, sc_ref[nb:nb + 1, :], sc_ref[pl.ds(b, 1), :])
                    shift = jnp.where(is_ctx, sh_ref[nb:nb + 1, :], sh_ref[pl.ds(b, 1), :])
                    a = (y * (1.0 + scale) + shift).astype(BF16)
                    a_ref[rows, :] = a
                else:
                    a = a_ref[rows, :]
                accs = [jnp.dot(a, w, preferred_element_type=F32) for w in ws]
                outs = epilogue(accs, par_refs)
                for o_ref, o in zip(out_refs, outs):
                    o_ref[rows, :] = o.astype(o_ref.dtype)

        @pl.when(pl.program_id(1) == 0)
        def _():
            compute(True)

        @pl.when(pl.program_id(1) > 0)
        def _():
            compute(False)

    return kern


def _src_specs(src, tm):
    d = src.h.shape[1]
    specs = [pl.BlockSpec((tm, d), lambda i, j: (i, 0), pipeline_mode=pl.Buffered(1)),
             pl.BlockSpec((None, 1, d), lambda i, j: (src.layer, 0, 0)),
             pl.BlockSpec((None, MOD_ROWS, d), lambda i, j: (src.layer, 0, src.g_shift)),
             pl.BlockSpec((None, MOD_ROWS, d), lambda i, j: (src.layer, 0, src.g_scale))]
    return specs, [src.h, src.norm_w, src.mods, src.mods]


class _Epilogue:
    def __init__(self, fn, n_out=1):
        self.fn = fn
        self.n_out = n_out

    def __call__(self, accs, par_refs):
        outs = self.fn(accs, par_refs)
        return outs if isinstance(outs, (tuple, list)) else (outs,)


def _fused_call(epilogue, src, tm, grid_inner, n_w, in_specs, args, out_shape, out_specs, name):
    m, d = src.h.shape
    specs, src_args = _src_specs(src, tm)
    return pl.pallas_call(
        _fused(epilogue, src, tm, n_w),
        out_shape=out_shape,
        grid=(m // tm, grid_inner),
        in_specs=specs + in_specs,
        out_specs=out_specs,
        scratch_shapes=[pltpu.VMEM((tm, d), BF16)],
        compiler_params=_params(("parallel", "arbitrary")),
        name=name,
    )(*src_args, *args)


def _mm_tiles(t, k, n, n_w, out_bytes, x_bytes=4, fixed=0, tm_target=1088, tn_target=512):
    tm = _row_tile(t, tm_target)
    while tm > 64:
        for tn in sorted({tn_target, 512, 256}, reverse=True):
            if tn > tn_target:
                continue
            need = tm * k * x_bytes + fixed + 2 * n_w * k * tn * 2 + 2 * tm * tn * out_bytes
            if n % tn == 0 and need <= VMEM_BLOCK_BUDGET:
                return tm, tn
        tm = _row_tile(t, tm - 8)
    return tm, _col_tile(n, 128)


def _fused_tiles(t, k, n, n_w, out_bytes, tn_target=1024):
    return _mm_tiles(t, k, n, n_w, out_bytes, x_bytes=6, fixed=8 * 1024 * 1024, tn_target=tn_target)


def _proj_call(epi_q, src, w, layer_idx, name, q_cols=0):
    m, k = src.h.shape
    n = w.shape[-1]
    tm, tn = _fused_tiles(src.dims[0], k, n, 1, 4)
    while q_cols % tn:
        tn //= 2

    def epi(accs, par_refs):
        acc = accs[0]
        if epi_q is None:
            return acc
        return jnp.where(pl.program_id(1) * tn < q_cols, epi_q(acc), acc)

    return _fused_call(
        _Epilogue(epi), src, tm, n // tn, 1,
        [pl.BlockSpec((None, k, tn), lambda i, j: (layer_idx, 0, j))], [w],
        jax.ShapeDtypeStruct((m, n), F32), pl.BlockSpec((tm, tn), lambda i, j: (i, j)), name)


def _hg_f_epilogue(accs, par_refs):
    z = jnp.clip(accs[0], -GATE_CLIP, GATE_CLIP)
    lb = par_refs[0][...]
    u = jnp.exp(-z)
    r = 1.0 / (1.0 + u)
    return (1.0 - lb) * (u * r), jnp.log((1.0 + lb * u) * r)


def _hg_f_call(src, w_f, layer_idx, lb):
    m, k = src.h.shape
    n = w_f.shape[-1]
    tm, tn = _fused_tiles(src.dims[0], k, n, 1, 8)
    nt = n // tn
    out = jax.ShapeDtypeStruct((2, m, n), F32)
    ospec = pl.BlockSpec((None, tm, tn), lambda i, j: (j // nt, i, j % nt))
    return _fused_call(
        _Epilogue(_hg_f_epilogue, 2), src, tm, 2 * nt, 1,
        [pl.BlockSpec((None, None, k, tn), lambda i, j: (layer_idx, j // nt, 0, j % nt)),
         pl.BlockSpec((1, tn), lambda i, j: (0, j % nt))],
        [w_f, lb.reshape(1, n)], (out, out), (ospec, ospec), "hgrn2_gates")


def _gla_gate_epilogue(accs, par_refs):
    w2_ref, b_ref = par_refs
    gl = jnp.dot(accs[0].astype(BF16), w2_ref[...].astype(BF16), preferred_element_type=F32) + b_ref[...]
    return _log_sigmoid(gl) * (1.0 / GLA_GATE_NORM)


def _gla_gate_call(src, w_g1, w_g2, b_g, layer_idx):
    m, k = src.h.shape
    rank, kw = w_g2.shape[-2:]
    tm = _row_tile(src.dims[0], 544)
    return _fused_call(
        _Epilogue(_gla_gate_epilogue), src, tm, 2, 1,
        [pl.BlockSpec((None, None, k, rank), lambda i, d: (layer_idx, d, 0, 0)),
         pl.BlockSpec((None, None, rank, kw), lambda i, d: (layer_idx, d, 0, 0)),
         pl.BlockSpec((None, None, 1, kw), lambda i, d: (layer_idx, d, 0, 0))],
        [w_g1, w_g2, b_g.reshape(b_g.shape[0], 2, 1, kw)],
        jax.ShapeDtypeStruct((2, m, kw), F32), pl.BlockSpec((None, tm, kw), lambda i, d: (d, i, 0)),
        "gla_gates")


def _dn_bg_epilogue(accs, par_refs):
    alog_ref, dtb_ref = par_refs
    beta = _sigmoid(accs[0])
    dt = _softplus(accs[1] + dtb_ref[...])
    return beta, -jnp.exp(alog_ref[...]) * dt


def _dn_bg_call(src, w_b, w_a, a_log, dt_bias, layer_idx):
    m, k = src.h.shape
    hv = w_b.shape[-1]
    tm = _row_tile(src.dims[0], 544)
    out = jax.ShapeDtypeStruct((2, m, hv), F32)
    wspec = pl.BlockSpec((None, None, k, hv), lambda i, d: (layer_idx, d, 0, 0))
    pspec = pl.BlockSpec((None, None, 1, hv), lambda i, d: (layer_idx, d, 0, 0))
    ospec = pl.BlockSpec((None, tm, hv), lambda i, d: (d, i, 0))
    nl = a_log.shape[0]
    return _fused_call(
        _Epilogue(_dn_bg_epilogue, 2), src, tm, 2, 2, [wspec, wspec, pspec, pspec],
        [w_b, w_a, a_log.reshape(nl, 2, 1, hv), dt_bias.reshape(nl, 2, 1, hv)],
        (out, out), (ospec, ospec), "deltanet_beta_decay")


def _swiglu_epilogue(accs, par_refs):
    return _silu(accs[0]) * accs[1]


def _swiglu_call(src, w_gate, w_up, layer):
    m, k = src.h.shape
    n = w_gate.shape[-1]
    tm, tn = _fused_tiles(src.dims[0], k, n, 2, 2, tn_target=512)
    wspec = pl.BlockSpec((None, k, tn), lambda i, j: (layer, 0, j))
    return _fused_call(
        _Epilogue(_swiglu_epilogue), src, tm, n // tn, 2, [wspec, wspec], [w_gate, w_up],
        jax.ShapeDtypeStruct((m, n), BF16), pl.BlockSpec((tm, tn), lambda i, j: (i, j)), "ffn_swiglu")


def _residual_kernel(x_ref, w_ref, h_ref, g_ref, o_ref, *, tm, tpb, ctx, nb):
    i = pl.program_id(0)
    b = i // tpb
    r0 = (i % tpb) * tm
    rc = _row_chunk(tm)
    w = w_ref[...].astype(BF16)
    for r in range(tm // rc):
        rows = slice(r * rc, (r + 1) * rc)
        is_ctx = lax.broadcasted_iota(jnp.int32, (rc, 1), 0) + (r0 + r * rc) < ctx
        gate = jnp.where(is_ctx, g_ref[nb:nb + 1, :], g_ref[pl.ds(b, 1), :])
        o_ref[rows, :] = h_ref[rows, :] + gate * jnp.dot(x_ref[rows, :], w, preferred_element_type=F32)


def _residual_call(y, w, w_idx, h, mods, layer, g_gate, dims, name):
    m, k = y.shape
    d = h.shape[1]
    t, ctx, nb = dims
    tm, tn = _mm_tiles(t, k, d, 1, 8)
    nt = d // tn
    kern = functools.partial(_residual_kernel, tm=tm, tpb=t // tm, ctx=ctx, nb=nb)
    return pl.pallas_call(
        kern,
        out_shape=jax.ShapeDtypeStruct(h.shape, F32),
        grid=(m // tm, nt),
        in_specs=[pl.BlockSpec((tm, k), lambda i, j: (i, 0)),
                  pl.BlockSpec((None, k, tn), lambda i, j: (w_idx, 0, j)),
                  pl.BlockSpec((tm, tn), lambda i, j: (i, j)),
                  pl.BlockSpec((None, MOD_ROWS, tn), lambda i, j: (layer, 0, g_gate * nt + j))],
        out_specs=pl.BlockSpec((tm, tn), lambda i, j: (i, j)),
        input_output_aliases={2: 0},
        compiler_params=_params(("parallel", "arbitrary")),
        name=name,
    )(y, w, h, mods)


def _gated_out_kernel(of_ref, ob_ref, g_ref, nw_ref, w_ref, h_ref, m_ref, o_ref, *, dv, tm, rc, tpb, ctx, nb):
    i = pl.program_id(0)
    b = i // tpb
    r0 = (i % tpb) * tm
    nw = nw_ref[...]
    w = w_ref[...].astype(BF16)
    for r in range(tm // rc):
        rows = slice(r * rc, (r + 1) * rc)
        ys = []
        for hh in range(of_ref.shape[1] // dv):
            cols = slice(hh * dv, (hh + 1) * dv)
            o = of_ref[rows, cols] + ob_ref[rows, cols]
            y = o * lax.rsqrt(jnp.mean(o * o, axis=-1, keepdims=True) + EPS) * nw
            ys.append((y * _silu(g_ref[rows, cols])).astype(BF16))
        y = ys[0] if len(ys) == 1 else jnp.concatenate(ys, axis=1)
        is_ctx = lax.broadcasted_iota(jnp.int32, (rc, 1), 0) + (r0 + r * rc) < ctx
        gate = jnp.where(is_ctx, m_ref[nb:nb + 1, :], m_ref[pl.ds(b, 1), :])
        o_ref[rows, :] = h_ref[rows, :] + gate * jnp.dot(y, w, preferred_element_type=F32)


def _gated_out_call(o_f, o_b, gate_arr, gate_col0, o_norm, w_o, w_idx, h, mods, layer, g_gate, dims, name):
    m, width = o_f.shape
    d = h.shape[1]
    t, ctx, nb = dims
    dv = o_norm.shape[-1]
    tm = _row_tile(t, 272)
    while tm > 8 and 3 * 2 * tm * width * 4 + width * d * 2 + 4 * tm * d * 4 > 40 * 1024 * 1024:
        tm = _row_tile(t, tm - 8)
    rc = _row_tile(tm, 136)
    kern = functools.partial(_gated_out_kernel, dv=dv, tm=tm, rc=rc, tpb=t // tm, ctx=ctx, nb=nb)
    row_spec = pl.BlockSpec((tm, width), lambda i: (i, 0))
    return pl.pallas_call(
        kern,
        out_shape=jax.ShapeDtypeStruct(h.shape, F32),
        grid=(m // tm,),
        in_specs=[row_spec, row_spec,
                  pl.BlockSpec((tm, width), lambda i: (i, gate_col0 // width)),
                  pl.BlockSpec((None, 1, dv), lambda i: (w_idx, 0, 0)),
                  pl.BlockSpec((None, width, d), lambda i: (w_idx, 0, 0), pipeline_mode=pl.Buffered(1)),
                  pl.BlockSpec((tm, d), lambda i: (i, 0)),
                  pl.BlockSpec((None, MOD_ROWS, d), lambda i: (layer, 0, g_gate))],
        out_specs=pl.BlockSpec((tm, d), lambda i: (i, 0)),
        input_output_aliases={5: 0},
        compiler_params=_params(("parallel",)),
        name=name,
    )(o_f, o_b, gate_arr, o_norm.reshape(o_norm.shape[0], 1, dv), w_o, h, mods)


def _chunk_cumsum(g, c, reverse):
    n = g.shape[0]
    pos = jnp.bitwise_and(lax.broadcasted_iota(jnp.int32, (n, 1), 0), c - 1)
    s = 1
    while s < c:
        if reverse:
            g = g + jnp.where(pos < c - s, pltpu.roll(g, n - s, 0), 0.0)
        else:
            g = g + jnp.where(pos >= s, pltpu.roll(g, s, 0), 0.0)
        s *= 2
    return g


def _bwd_group(s, ng):
    return jnp.where(s == 0, 0, ng - s)


def _dot_nt(a, b):
    return lax.dot_general(a, b, (((1,), (1,)), ((), ())), preferred_element_type=F32)


def _gla_scan_kernel(qf_ref, kf_ref, vf_ref, gf_ref, qb_ref, kb_ref, vb_ref, gb_ref,
                     of_ref, ob_ref, st_ref, b_ref, *, gt, dk, dv, hpb):
    c = CHUNK
    nch = gt // c
    width = hpb * dk

    @pl.when(pl.program_id(2) == 0)
    def _():
        st_ref[...] = jnp.zeros_like(st_ref)

    crow = lax.broadcasted_iota(jnp.int32, (c, c), 0)
    ccol = lax.broadcasted_iota(jnp.int32, (c, c), 1)
    masks = (ccol <= crow, ccol >= crow)
    mid = (c // 2 - 1, c // 2)
    end = (c - 1, 0)
    g_refs = (gf_ref, gb_ref)
    q_refs = (qf_ref, qb_ref)
    k_refs = (kf_ref, kb_ref)
    v_refs = (vf_ref, vb_ref)
    o_refs = (of_ref, ob_ref)

    worst = jnp.zeros((1, width), F32)
    mag = jnp.ones((1, width), F32)
    for d in range(2):
        mag = jnp.maximum(mag, jnp.max(jnp.maximum(jnp.abs(q_refs[d][...]), jnp.abs(k_refs[d][...])),
                                       axis=0, keepdims=True))
        b = _chunk_cumsum(g_refs[d][...], c, reverse=d == 1)
        b_ref[d] = b
        for ci in range(nch):
            r = b[ci * c + mid[d]:ci * c + mid[d] + 1, :]
            e = b[ci * c + end[d]:ci * c + end[d] + 1, :]
            worst = jnp.maximum(worst, jnp.maximum(-r, r - e))
    safe = jnp.max(worst + jnp.log(mag)) <= EXP_RANGE

    def scores_fast(d, ci, q, k, b, kcols):
        r = b[mid[d]:mid[d] + 1, :]
        qs = (q * jnp.exp(b - r)).astype(BF16)
        ks = (k * jnp.exp(r - b)).astype(BF16)
        return jnp.where(masks[d], _dot_nt(qs, ks), 0.0)

    def scores_exact(d, ci, q, k, b, kcols):
        lane = lax.broadcasted_iota(jnp.int32, (c, c), 1)

        def body(s, acc):
            ks = k_refs[d][pl.ds(ci * c + s, 1), :][:, kcols]
            bs = b_ref[d, pl.ds(ci * c + s, 1), :][:, kcols]
            term = q * ks * jnp.exp(jnp.minimum(b - bs, 0.0))
            return jnp.where(lane == s, jnp.sum(term, axis=1, keepdims=True), acc)

        acc = lax.fori_loop(0, c, body, jnp.zeros((c, c), F32))
        return jnp.where(masks[d], acc, 0.0)

    def run(scores):
        pre = {}
        for hh in range(hpb):
            kcols = slice(hh * dk, (hh + 1) * dk)
            vcols = slice(hh * dv, (hh + 1) * dv)
            for d in range(2):
                for ci in range(nch):
                    rows = slice(ci * c, (ci + 1) * c)
                    q = q_refs[d][rows, kcols]
                    k = k_refs[d][rows, kcols]
                    v = v_refs[d][rows, vcols]
                    b = b_ref[d, rows, kcols]
                    b_end = b[end[d]:end[d] + 1, :]
                    a = scores(d, ci, q, k, b, kcols).astype(BF16)
                    ke = (k * jnp.exp(b_end - b)).astype(BF16)
                    pre[hh, d, ci] = dict(
                        av=jnp.dot(a, v.astype(BF16), preferred_element_type=F32),
                        qe=(q * jnp.exp(b)).astype(BF16),
                        upd=jnp.dot(v.T.astype(BF16), ke, preferred_element_type=F32),
                        dec=jnp.exp(b_end))
        for hh in range(hpb):
            vcols = slice(hh * dv, (hh + 1) * dv)
            for d in range(2):
                st = st_ref[d, hh]
                for step in range(nch):
                    ci = step if d == 0 else nch - 1 - step
                    p = pre[hh, d, ci]
                    o_refs[d][ci * c:(ci + 1) * c, vcols] = p["av"] + _dot_nt(p["qe"], st.astype(BF16))
                    st = st * p["dec"] + p["upd"]
                st_ref[d, hh] = st

    @pl.when(safe)
    def _():
        run(scores_fast)

    @pl.when(jnp.logical_not(safe))
    def _():
        run(scores_exact)


def _gla_scan_call(q_src, k_srcs, v_src, g_arr, nb, t, gt, heads, dk, dv):
    ng = t // gt
    hpb = 2 if heads % 2 == 0 else 1
    dkb, dvb = hpb * dk, hpb * dv

    def spec3(col0, width, bwd):
        off = col0 // width
        if bwd:
            return pl.BlockSpec((None, gt, width), lambda b, h, s: (b, _bwd_group(s, ng), off + h))
        return pl.BlockSpec((None, gt, width), lambda b, h, s: (b, s, off + h))

    def spec4(lead, col0, width, bwd):
        off = col0 // width
        if bwd:
            return pl.BlockSpec((None, None, gt, width),
                                lambda b, h, s: (lead, b, _bwd_group(s, ng), off + h))
        return pl.BlockSpec((None, None, gt, width), lambda b, h, s: (lead, b, s, off + h))

    def kspec(src, bwd):
        arr, lead, col0 = src
        return spec3(col0, dkb, bwd) if lead is None else spec4(lead, col0, dkb, bwd)

    in_specs, args = [], []
    for d in range(2):
        bwd = d == 1
        in_specs += [spec3(q_src[1], dkb, bwd), kspec(k_srcs[d], bwd), spec3(v_src[1], dvb, bwd),
                     spec4(d, 0, dkb, bwd)]
        args += [q_src[0], k_srcs[d][0], v_src[0], g_arr]
    out = jax.ShapeDtypeStruct((nb, t, heads * dv), F32)
    kern = functools.partial(_gla_scan_kernel, gt=gt, dk=dk, dv=dv, hpb=hpb)
    return pl.pallas_call(
        kern,
        out_shape=(out, out),
        grid=(nb, heads // hpb, ng),
        in_specs=in_specs,
        out_specs=(pl.BlockSpec((None, gt, dvb), lambda b, h, s: (b, s, h)),
                   pl.BlockSpec((None, gt, dvb), lambda b, h, s: (b, _bwd_group(s, ng), h))),
        scratch_shapes=[pltpu.VMEM((2, hpb, dv, dk), F32), pltpu.VMEM((2, gt, dkb), F32)],
        compiler_params=_params(("parallel", "parallel", "arbitrary")),
        name="gla_scan",
    )(*args)


def _dn_conv_kernel(x_ref, w_ref, o_ref, *, gt, n_norm_tiles, n_q_tiles, qscale):
    s = pl.program_id(1)
    j = pl.program_id(2)
    x = x_ref[...]
    w = w_ref[...]
    seg = jnp.where(s == 0, gt, GRID_W)
    r = lax.broadcasted_iota(jnp.int32, (gt, 1), 0)
    pos = jnp.where(s == 0, r, jnp.bitwise_and(r, GRID_W - 1))
    acc = x * w[CONV_K // 2:CONV_K // 2 + 1, :]
    for tap in range(CONV_K):
        off = tap - CONV_K // 2
        if off == 0:
            continue
        shifted = pltpu.roll(x, (-off) % gt, 0)
        ok = (pos + off >= 0) & (pos + off < seg)
        acc = acc + jnp.where(ok, shifted, 0.0) * w[tap:tap + 1, :]
    y = _silu(acc)
    o_ref[...] = y

    @pl.when(j < n_norm_tiles)
    def _():
        scale = jnp.where(j < n_q_tiles, qscale, 1.0)
        for hh in range(y.shape[1] // HEAD_DIM):
            cols = slice(hh * HEAD_DIM, (hh + 1) * HEAD_DIM)
            yh = y[:, cols]
            o_ref[:, cols] = yh * (lax.rsqrt(jnp.sum(yh * yh, axis=-1, keepdims=True) + EPS) * scale)


def _dn_conv_call(p, conv_w, layer_idx, nb, t, gt, qkv_w, key_w):
    tc = min(512, key_w)
    kern = functools.partial(_dn_conv_kernel, gt=gt, n_norm_tiles=2 * key_w // tc,
                             n_q_tiles=key_w // tc, qscale=HEAD_DIM ** -0.5)
    return pl.pallas_call(
        kern,
        out_shape=jax.ShapeDtypeStruct((nb, t, qkv_w), F32),
        grid=(nb, t // gt, qkv_w // tc),
        in_specs=[pl.BlockSpec((None, gt, tc), lambda b, s, j: (b, s, j)),
                  pl.BlockSpec((None, CONV_K, tc), lambda b, s, j: (layer_idx, 0, j))],
        out_specs=pl.BlockSpec((None, gt, tc), lambda b, s, j: (b, s, j)),
        compiler_params=_params(("parallel", "parallel", "parallel")),
        name="deltanet_conv",
    )(p, conv_w)


def _mm3(a, b):
    a1 = a.astype(BF16)
    a2 = (a - a1.astype(F32)).astype(BF16)
    b1 = b.astype(BF16)
    b2 = (b - b1.astype(F32)).astype(BF16)
    return (jnp.dot(a1, b1, preferred_element_type=F32) + jnp.dot(a1, b2, preferred_element_type=F32)
            + jnp.dot(a2, b1, preferred_element_type=F32))


def _mm1(a, b):
    return jnp.dot(a.astype(BF16), b.astype(BF16), preferred_element_type=F32)


INVERSE_EXACT_LEVELS = 0


def _neumann_inverses(ms, eye):
    c = CHUNK
    ps = [eye + m for m in ms]
    xs = [_mm3(m, m) for m in ms]
    levels = c.bit_length() - 2
    for lvl in range(levels):
        mm = _mm3 if lvl < INVERSE_EXACT_LEVELS else _mm1
        if lvl == levels - 1:
            ps = [p + mm(p, x) for p, x in zip(ps, xs)]
        else:
            rs = [mm(jnp.concatenate([x, p], axis=0), x) for p, x in zip(ps, xs)]
            xs = [r[:c] for r in rs]
            ps = [p + r[c:] for p, r in zip(ps, rs)]
    return ps


def _dn_scan_kernel(qf_ref, kf_ref, vf_ref, qb_ref, kb_ref, vb_ref, beta_f_ref, ld_f_ref,
                    beta_b_ref, ld_b_ref, of_ref, ob_ref, st_ref, *, gt, rep):
    c = CHUNK
    nch = gt // c

    @pl.when(pl.program_id(2) == 0)
    def _():
        st_ref[...] = jnp.zeros_like(st_ref)

    row = lax.broadcasted_iota(jnp.int32, (c, c), 0)
    col = lax.broadcasted_iota(jnp.int32, (c, c), 1)
    diag = row == col
    eye = jnp.where(diag, 1.0, 0.0)
    incl = (col <= row, col >= row)
    strict = (col < row, col > row)
    end = (c - 1, 0)
    q_refs = (qf_ref, qb_ref)
    k_refs = (kf_ref, kb_ref)
    v_refs = (vf_ref, vb_ref)
    o_refs = (of_ref, ob_ref)
    beta_refs = (beta_f_ref, beta_b_ref)
    ld_refs = (ld_f_ref, ld_b_ref)

    def to_col(x_row):
        return jnp.sum(jnp.where(diag, x_row, 0.0), axis=1, keepdims=True)

    units = {}
    for d in range(2):
        for ci in range(nch):
            rows = slice(ci * c, (ci + 1) * c)
            q = q_refs[d][rows, :]
            k = k_refs[d][rows, :]
            qb16 = q.astype(BF16)
            kb16 = k.astype(BF16)
            kk = _dot_nt(kb16, kb16)
            qk = _dot_nt(qb16, kb16)
            for r in range(rep):
                beta_row = beta_refs[d][r, ci:ci + 1, :]
                g_row = ld_refs[d][r, ci:ci + 1, :]
                g_col = to_col(g_row)
                beta_col = to_col(beta_row)
                gc_col = jnp.sum(jnp.where(incl[d], g_row, 0.0), axis=1, keepdims=True)
                gc_row = jnp.sum(jnp.where(incl[1 - d], g_col, 0.0), axis=0, keepdims=True)
                dec = jnp.exp(jnp.minimum(gc_col - gc_row, 0.0))
                gc_end = gc_col[end[d]:end[d] + 1, :]
                eg = jnp.exp(gc_col)
                v = v_refs[d][rows, r * HEAD_DIM:(r + 1) * HEAD_DIM]
                units[d, ci, r] = dict(
                    m=jnp.where(strict[d], -(beta_col * kk * dec), 0.0),
                    rhs_v=(beta_col * v).astype(BF16),
                    rhs_k=(beta_col * eg * k).astype(BF16),
                    a=jnp.where(incl[d], qk * dec, 0.0).astype(BF16),
                    kd_t=(k * jnp.exp(gc_end - gc_col)).T.astype(BF16),
                    eg=eg, dec_end=jnp.exp(gc_end), qb16=qb16)

    keys = list(units)
    for key, tinv in zip(keys, _neumann_inverses([units[key]["m"] for key in keys], eye)):
        u = units[key]
        sol = jnp.dot(tinv.astype(BF16), jnp.concatenate([u["rhs_v"], u["rhs_k"]], axis=1),
                      preferred_element_type=F32)
        u["sol_v"] = sol[:, :HEAD_DIM]
        u["kq"] = jnp.concatenate([sol[:, HEAD_DIM:].astype(BF16), u["qb16"]], axis=0)
        u["a_kd"] = jnp.concatenate([u["a"], u["kd_t"]], axis=0)

    chains = [(d, r) for d in range(2) for r in range(rep)]
    states = {ch: st_ref[ch[0], ch[1]] for ch in chains}
    for step in range(nch):
        for d, r in chains:
            ci = step if d == 0 else nch - 1 - step
            u = units[d, ci, r]
            st = states[d, r]
            ks_qs = jnp.dot(u["kq"], st.astype(BF16), preferred_element_type=F32)
            w16 = (u["sol_v"] - ks_qs[:c]).astype(BF16)
            aw_kw = jnp.dot(u["a_kd"], w16, preferred_element_type=F32)
            o_refs[d][ci * c:(ci + 1) * c, r * HEAD_DIM:(r + 1) * HEAD_DIM] = u["eg"] * ks_qs[c:] + aw_kw[:c]
            states[d, r] = st * u["dec_end"] + aw_kw[c:]
    for d, r in chains:
        st_ref[d, r] = states[d, r]


def _dn_scan_call(qkv, beta, ld, nb, t, gt, key_w, rep):
    ng = t // gt
    hd = HEAD_DIM
    hq = key_w // hd
    nch = gt // CHUNK
    koff = key_w // hd
    voff = 2 * key_w // (rep * hd)

    def spec(width, off, bwd):
        if bwd:
            return pl.BlockSpec((None, gt, width), lambda b, h, s: (b, _bwd_group(s, ng), off + h))
        return pl.BlockSpec((None, gt, width), lambda b, h, s: (b, s, off + h))

    def sspec(d):
        if d == 1:
            return pl.BlockSpec((None, None, rep, None, nch, CHUNK),
                                lambda b, h, s: (1, b, h, _bwd_group(s, ng), 0, 0))
        return pl.BlockSpec((None, None, rep, None, nch, CHUNK), lambda b, h, s: (0, b, h, s, 0, 0))

    in_specs, args = [], []
    for d in range(2):
        in_specs += [spec(hd, 0, d == 1), spec(hd, koff, d == 1), spec(rep * hd, voff, d == 1)]
        args += [qkv, qkv, qkv]
    in_specs += [sspec(0), sspec(0), sspec(1), sspec(1)]
    args += [beta, ld, beta, ld]
    out = jax.ShapeDtypeStruct((nb, t, hq * rep * hd), F32)
    kern = functools.partial(_dn_scan_kernel, gt=gt, rep=rep)
    return pl.pallas_call(
        kern,
        out_shape=(out, out),
        grid=(nb, hq, ng),
        in_specs=in_specs,
        out_specs=(pl.BlockSpec((None, gt, rep * hd), lambda b, h, s: (b, s, h)),
                   pl.BlockSpec((None, gt, rep * hd), lambda b, h, s: (b, _bwd_group(s, ng), h))),
        scratch_shapes=[pltpu.VMEM((2, rep, hd, hd), F32)],
        compiler_params=_params(("parallel", "parallel", "arbitrary")),
        name="deltanet_scan",
    )(*args)


def _final_norm_kernel(h_ref, w_ref, o_ref):
    x = h_ref[...]
    o_ref[...] = x * lax.rsqrt(jnp.mean(x * x, axis=-1, keepdims=True) + EPS) * w_ref[...]


def _final_norm_call(h3, w, ctx, seq):
    nb, t, d = h3.shape
    skip = ctx // ctx
    return pl.pallas_call(
        _final_norm_kernel,
        out_shape=jax.ShapeDtypeStruct((nb, seq, d), F32),
        grid=(nb, seq // ctx),
        in_specs=[pl.BlockSpec((None, ctx, d), lambda b, i: (b, i + skip, 0)),
                  pl.BlockSpec((1, d), lambda b, i: (0, 0))],
        out_specs=pl.BlockSpec((None, ctx, d), lambda b, i: (b, i, 0)),
        compiler_params=_params(("parallel", "parallel")),
        name="final_norm",
    )(h3, w.reshape(1, d))


def _scan_scalars(x, nb, t, gt):
    hv = x.shape[-1]
    x = x.reshape(2, nb, t, hv).transpose(0, 1, 3, 2)
    return x.reshape(2, nb, hv, t // gt, gt // CHUNK, CHUNK)


def kernel(x, c, ctx, c_ctx, w_ada, b_ada, norm_mix, norm_ffn, hg_lb_logits, hg_w_in, hg_w_f, hg_o_norm, hg_w_o, gla_w_in, gla_w_g1, gla_w_g2, gla_b_g, gla_o_norm, gla_w_o, dn_w_in, dn_conv, dn_w_b, dn_w_a, dn_a_log, dn_dt_bias, dn_o_norm, dn_w_o, ffn_w_gate, ffn_w_up, ffn_w_down, final_norm):
    nb, seq, d = x.shape
    n_ctx = ctx.shape[1]
    depth = w_ada.shape[0]
    t = n_ctx + seq
    gt = n_ctx
    assert n_ctx % CHUNK == 0 and seq % gt == 0 and GRID_W == CHUNK and nb + 1 <= MOD_ROWS
    dims = (t, n_ctx, nb)
    m = nb * t

    (hg_w_in, hg_w_f, hg_w_o, gla_w_in, gla_w_g1, gla_w_g2, gla_w_o, dn_w_in, dn_w_b, dn_w_a, dn_w_o,
     ffn_w_gate, ffn_w_up, ffn_w_down) = (
        w.astype(BF16) for w in (hg_w_in, hg_w_f, hg_w_o, gla_w_in, gla_w_g1, gla_w_g2, gla_w_o, dn_w_in,
                                 dn_w_b, dn_w_a, dn_w_o, ffn_w_gate, ffn_w_up, ffn_w_down))

    p = jax.nn.softmax(hg_lb_logits.astype(F32), axis=0)
    lower_bounds = jnp.cumsum(p, axis=0) - p[0]

    cc = jnp.zeros((MOD_ROWS, d), F32).at[:nb].set(c).at[nb].set(c_ctx)
    mods = _ada_call(cc, w_ada, b_ada)

    h = jnp.concatenate([ctx, x], axis=1).reshape(m, d)

    hg_heads = d // HEAD_DIM
    hg_kw = hg_w_f.shape[-1]
    gla_kw = gla_w_g2.shape[-1]
    gla_dk = gla_kw // GLA_HEADS
    gla_dv = d // GLA_HEADS
    dn_key_w = d
    dn_hv = dn_w_b.shape[-1]
    dn_val_w = dn_hv * HEAD_DIM
    dn_qkv_w = 2 * dn_key_w + dn_val_w
    rep = dn_hv // (dn_key_w // HEAD_DIM)

    for i in range(depth):
        kind, j = i % 3, i // 3
        src = _NormSrc(h, norm_mix.reshape(depth, 1, d), mods, i, 0, 1, dims)
        if kind == 0:
            proj = _proj_call(lambda acc: _silu(acc) * HEAD_DIM ** -0.5, src, hg_w_in, j,
                              "hgrn2_in_proj", hg_kw)
            kk, lf = _hg_f_call(src, hg_w_f, j, lower_bounds[i])
            p3 = proj.reshape(nb, t, -1)
            kk4 = kk.reshape(2, nb, t, hg_kw)
            o_f, o_b = _gla_scan_call((p3, 0), ((kk4, 0, 0), (kk4, 1, 0)), (p3, hg_kw),
                                      lf.reshape(2, nb, t, hg_kw), nb, t, gt, hg_heads, HEAD_DIM,
                                      d // hg_heads)
            h = _gated_out_call(o_f.reshape(m, -1), o_b.reshape(m, -1), proj, hg_kw + d, hg_o_norm, hg_w_o, j,
                                h, mods, i, 2, dims, "hgrn2_out_proj")
        elif kind == 1:
            proj = _proj_call(lambda acc: acc * gla_dk ** -0.5, src, gla_w_in, j,
                              "gla_in_proj", gla_kw)
            lg = _gla_gate_call(src, gla_w_g1, gla_w_g2, gla_b_g, j)
            p3 = proj.reshape(nb, t, -1)
            o_f, o_b = _gla_scan_call((p3, 0), ((p3, None, gla_kw), (p3, None, gla_kw)), (p3, 2 * gla_kw),
                                      lg.reshape(2, nb, t, gla_kw), nb, t, gt, GLA_HEADS, gla_dk, gla_dv)
            h = _gated_out_call(o_f.reshape(m, -1), o_b.reshape(m, -1), proj, 2 * gla_kw + d, gla_o_norm,
                                gla_w_o, j, h, mods, i, 2, dims, "gla_out_proj")
        else:
            proj = _proj_call(None, src, dn_w_in, j, "deltanet_in_proj")
            beta, ld = _dn_bg_call(src, dn_w_b, dn_w_a, dn_a_log, dn_dt_bias, j)
            qkv = _dn_conv_call(proj.reshape(nb, t, -1), dn_conv, j, nb, t, gt, dn_qkv_w, dn_key_w)
            o_f, o_b = _dn_scan_call(qkv, _scan_scalars(beta, nb, t, gt), _scan_scalars(ld, nb, t, gt),
                                     nb, t, gt, dn_key_w, rep)
            h = _gated_out_call(o_f.reshape(m, -1), o_b.reshape(m, -1), proj, dn_qkv_w, dn_o_norm, dn_w_o, j,
                                h, mods, i, 2, dims, "deltanet_out_proj")
        src = _NormSrc(h, norm_ffn.reshape(depth, 1, d), mods, i, 3, 4, dims)
        u = _swiglu_call(src, ffn_w_gate, ffn_w_up, i)
        h = _residual_call(u, ffn_w_down, i, h, mods, i, 5, dims, "ffn_down_proj")

    return _final_norm_call(h.reshape(nb, t, d), final_norm, n_ctx, seq)
```

```python
import functools
from typing import NamedTuple

import jax
import jax.numpy as jnp
from jax import lax
from jax.experimental import pallas as pl
from jax.experimental.pallas import tpu as pltpu

F32 = jnp.float32
BF16 = jnp.bfloat16

EPS = 1e-6
GATE_CLIP = 30.0
CHUNK = 64
GRID_W = 64
CONV_K = 5
GLA_HEADS = 4
GLA_GATE_NORM = 16.0
HEAD_DIM = 128
EXP_RANGE = 80.0

V7X_VMEM_LIMIT = 56 * 1024 * 1024
VMEM_BLOCK_BUDGET = 44 * 1024 * 1024
BF16_SUBLANES = 16
MOD_ROWS = 8


def _params(sem):
    return pltpu.CompilerParams(dimension_semantics=sem, vmem_limit_bytes=V7X_VMEM_LIMIT)


def _sigmoid(x):
    return 1.0 / (1.0 + jnp.exp(-x))


def _silu(x):
    return x * _sigmoid(x)


def _softplus(x):
    return jnp.maximum(x, 0.0) + jnp.log1p(jnp.exp(-jnp.abs(x)))


def _log_sigmoid(x):
    return -_softplus(-x)


def _row_tile(t, target):
    best = 8
    for d in range(8, min(t, target) + 1, 8):
        if t % d == 0:
            best = d
    return best


def _col_tile(n, target):
    for c in (target, 512, 256, 128):
        if c <= target and n % c == 0:
            return c
    return n


def _mod_rows(m_ref, i, tm, tiles_per_batch, ctx, nb):
    b = i // tiles_per_batch
    r0 = (i % tiles_per_batch) * tm
    lat = m_ref[pl.ds(b, 1), :]
    cx = m_ref[nb:nb + 1, :]
    rows = lax.broadcasted_iota(jnp.int32, (tm, 1), 0) + r0
    return jnp.where(rows < ctx, cx, lat)


def _ada_kernel(c_ref, w_ref, b_ref, o_ref):
    s = _silu(c_ref[...]).astype(BF16)
    o_ref[...] = jnp.dot(s, w_ref[...].astype(BF16), preferred_element_type=F32) + b_ref[...]


def _ada_call(cc, w_ada, b_ada):
    depth, d, n = w_ada.shape
    tn = _col_tile(n, 1024)
    return pl.pallas_call(
        _ada_kernel,
        out_shape=jax.ShapeDtypeStruct((depth, MOD_ROWS, n), F32),
        grid=(depth, n // tn),
        in_specs=[pl.BlockSpec((MOD_ROWS, d), lambda i, j: (0, 0)),
                  pl.BlockSpec((None, d, tn), lambda i, j: (i, 0, j)),
                  pl.BlockSpec((None, 1, tn), lambda i, j: (i, 0, j))],
        out_specs=pl.BlockSpec((None, MOD_ROWS, tn), lambda i, j: (i, 0, j)),
        compiler_params=_params(("parallel", "parallel")),
        name="ada_mods",
    )(cc, w_ada, b_ada.reshape(depth, 1, n))


class _NormSrc(NamedTuple):
    h: jax.Array
    norm_w: jax.Array
    mods: jax.Array
    layer: int
    g_shift: int
    g_scale: int
    dims: tuple


def _dot(x, w_ref):
    return jnp.dot(x, w_ref[...].astype(BF16), preferred_element_type=F32)


def _row_chunk(tm):
    best = tm
    for rc in range(BF16_SUBLANES, min(tm, 272) + 1, BF16_SUBLANES):
        if tm % rc == 0:
            best = rc
    return best


def _fused(epilogue, src, tm, n_w):
    t, ctx, nb = src.dims
    tpb = t // tm
    rc = _row_chunk(tm)

    def kern(h_ref, nw_ref, sh_ref, sc_ref, *rest):
        a_ref = rest[-1]
        w_refs = rest[:n_w]
        n_out = epilogue.n_out
        par_refs = rest[n_w:len(rest) - 1 - n_out]
        out_refs = rest[len(rest) - 1 - n_out:-1]
        i = pl.program_id(0)
        b = i // tpb
        r0 = (i % tpb) * tm

        def compute(first):
            ws = [w[...].astype(BF16) for w in w_refs]
            for r in range(tm // rc):
                rows = slice(r * rc, (r + 1) * rc)
                if first:
                    x = h_ref[rows, :]
                    y = x * lax.rsqrt(jnp.mean(x * x, axis=-1, keepdims=True) + EPS) * nw_ref[...]
                    is_ctx = lax.broadcasted_iota(jnp.int32, (rc, 1), 0) + (r0 + r * rc) < ctx
                    scale = jnp.where(is_ctx, sc_ref[nb:nb + 1, :], sc_ref[pl.ds(b, 1), :])
                    shift = jnp.where(is_ctx, sh_ref[nb:nb + 1, :], sh_ref[pl.ds(b, 1), :])
                    a = (y * (1.0 + scale) + shift).astype(BF16)
                    a_ref[rows, :] = a
                else:
                    a = a_ref[rows, :]
                accs = [jnp.dot(a, w, preferred_element_type=F32) for w in ws]
                outs = epilogue(accs, par_refs)
                for o_ref, o in zip(out_refs, outs):
                    o_ref[rows, :] = o.astype(o_ref.dtype)

        @pl.when(pl.program_id(1) == 0)
        def _():
            compute(True)

        @pl.when(pl.program_id(1) > 0)
        def _():
            compute(False)

    return kern


def _src_specs(src, tm):
    d = src.h.shape[1]
    specs = [pl.BlockSpec((tm, d), lambda i, j: (i, 0), pipeline_mode=pl.Buffered(1)),
             pl.BlockSpec((None, 1, d), lambda i, j: (src.layer, 0, 0)),
             pl.BlockSpec((None, MOD_ROWS, d), lambda i, j: (src.layer, 0, src.g_shift)),
             pl.BlockSpec((None, MOD_ROWS, d), lambda i, j: (src.layer, 0, src.g_scale))]
    return specs, [src.h, src.norm_w, src.mods, src.mods]


class _Epilogue:
    def __init__(self, fn, n_out=1):
        self.fn = fn
        self.n_out = n_out

    def __call__(self, accs, par_refs):
        outs = self.fn(accs, par_refs)
        return outs if isinstance(outs, (tuple, list)) else (outs,)


def _fused_call(epilogue, src, tm, grid_inner, n_w, in_specs, args, out_shape, out_specs, name):
    m, d = src.h.shape
    specs, src_args = _src_specs(src, tm)
    return pl.pallas_call(
        _fused(epilogue, src, tm, n_w),
        out_shape=out_shape,
        grid=(m // tm, grid_inner),
        in_specs=specs + in_specs,
        out_specs=out_specs,
        scratch_shapes=[pltpu.VMEM((tm, d), BF16)],
        compiler_params=_params(("parallel", "arbitrary")),
        name=name,
    )(*src_args, *args)


def _mm_tiles(t, k, n, n_w, out_bytes, x_bytes=4, fixed=0, tm_target=1088, tn_target=512):
    tm = _row_tile(t, tm_target)
    while tm > 64:
        for tn in sorted({tn_target, 512, 256}, reverse=True):
            if tn > tn_target:
                continue
            need = tm * k * x_bytes + fixed + 2 * n_w * k * tn * 2 + 2 * tm * tn * out_bytes
            if n % tn == 0 and need <= VMEM_BLOCK_BUDGET:
                return tm, tn
        tm = _row_tile(t, tm - 8)
    return tm, _col_tile(n, 128)


def _fused_tiles(t, k, n, n_w, out_bytes, tn_target=1024):
    return _mm_tiles(t, k, n, n_w, out_bytes, x_bytes=6, fixed=8 * 1024 * 1024, tn_target=tn_target)


def _proj_call(epi_q, src, w, layer_idx, name, q_cols=0):
    m, k = src.h.shape
    n = w.shape[-1]
    tm, tn = _fused_tiles(src.dims[0], k, n, 1, 4)
    while q_cols % tn:
        tn //= 2

    def epi(accs, par_refs):
        acc = accs[0]
        if epi_q is None:
            return acc
        return jnp.where(pl.program_id(1) * tn < q_cols, epi_q(acc), acc)

    return _fused_call(
        _Epilogue(epi), src, tm, n // tn, 1,
        [pl.BlockSpec((None, k, tn), lambda i, j: (layer_idx, 0, j))], [w],
        jax.ShapeDtypeStruct((m, n), F32), pl.BlockSpec((tm, tn), lambda i, j: (i, j)), name)


def _hg_f_epilogue(accs, par_refs):
    z = jnp.clip(accs[0], -GATE_CLIP, GATE_CLIP)
    lb = par_refs[0][...]
    u = jnp.exp(-z)
    r = 1.0 / (1.0 + u)
    return (1.0 - lb) * (u * r), jnp.log((1.0 + lb * u) * r)


def _hg_f_call(src, w_f, layer_idx, lb):
    m, k = src.h.shape
    n = w_f.shape[-1]
    tm, tn = _fused_tiles(src.dims[0], k, n, 1, 8)
    nt = n // tn
    out = jax.ShapeDtypeStruct((2, m, n), F32)
    ospec = pl.BlockSpec((None, tm, tn), lambda i, j: (j // nt, i, j % nt))
    return _fused_call(
        _Epilogue(_hg_f_epilogue, 2), src, tm, 2 * nt, 1,
        [pl.BlockSpec((None, None, k, tn), lambda i, j: (layer_idx, j // nt, 0, j % nt)),
         pl.BlockSpec((1, tn), lambda i, j: (0, j % nt))],
        [w_f, lb.reshape(1, n)], (out, out), (ospec, ospec), "hgrn2_gates")


def _gla_gate_epilogue(accs, par_refs):
    w2_ref, b_ref = par_refs
    gl = jnp.dot(accs[0].astype(BF16), w2_ref[...].astype(BF16), preferred_element_type=F32) + b_ref[...]
    return _log_sigmoid(gl) * (1.0 / GLA_GATE_NORM)


def _gla_gate_call(src, w_g1, w_g2, b_g, layer_idx):
    m, k = src.h.shape
    rank, kw = w_g2.shape[-2:]
    tm = _row_tile(src.dims[0], 544)
    return _fused_call(
        _Epilogue(_gla_gate_epilogue), src, tm, 2, 1,
        [pl.BlockSpec((None, None, k, rank), lambda i, d: (layer_idx, d, 0, 0)),
         pl.BlockSpec((None, None, rank, kw), lambda i, d: (layer_idx, d, 0, 0)),
         pl.BlockSpec((None, None, 1, kw), lambda i, d: (layer_idx, d, 0, 0))],
        [w_g1, w_g2, b_g.reshape(b_g.shape[0], 2, 1, kw)],
        jax.ShapeDtypeStruct((2, m, kw), F32), pl.BlockSpec((None, tm, kw), lambda i, d: (d, i, 0)),
        "gla_gates")


def _dn_bg_epilogue(accs, par_refs):
    alog_ref, dtb_ref = par_refs
    beta = _sigmoid(accs[0])
    dt = _softplus(accs[1] + dtb_ref[...])
    return beta, -jnp.exp(alog_ref[...]) * dt


def _dn_bg_call(src, w_b, w_a, a_log, dt_bias, layer_idx):
    m, k = src.h.shape
    hv = w_b.shape[-1]
    tm = _row_tile(src.dims[0], 544)
    out = jax.ShapeDtypeStruct((2, m, hv), F32)
    wspec = pl.BlockSpec((None, None, k, hv), lambda i, d: (layer_idx, d, 0, 0))
    pspec = pl.BlockSpec((None, None, 1, hv), lambda i, d: (layer_idx, d, 0, 0))
    ospec = pl.BlockSpec((None, tm, hv), lambda i, d: (d, i, 0))
    nl = a_log.shape[0]
    return _fused_call(
        _Epilogue(_dn_bg_epilogue, 2), src, tm, 2, 2, [wspec, wspec, pspec, pspec],
        [w_b, w_a, a_log.reshape(nl, 2, 1, hv), dt_bias.reshape(nl, 2, 1, hv)],
        (out, out), (ospec, ospec), "deltanet_beta_decay")


def _swiglu_epilogue(accs, par_refs):
    return _silu(accs[0]) * accs[1]


def _swiglu_call(src, w_gate, w_up, layer):
    m, k = src.h.shape
    n = w_gate.shape[-1]
    tm, tn = _fused_tiles(src.dims[0], k, n, 2, 2, tn_target=512)
    wspec = pl.BlockSpec((None, k, tn), lambda i, j: (layer, 0, j))
    return _fused_call(
        _Epilogue(_swiglu_epilogue), src, tm, n // tn, 2, [wspec, wspec], [w_gate, w_up],
        jax.ShapeDtypeStruct((m, n), BF16), pl.BlockSpec((tm, tn), lambda i, j: (i, j)), "ffn_swiglu")


def _residual_kernel(x_ref, w_ref, h_ref, g_ref, o_ref, *, tm, tpb, ctx, nb):
    i = pl.program_id(0)
    b = i // tpb
    r0 = (i % tpb) * tm
    rc = _row_chunk(tm)
    w = w_ref[...].astype(BF16)
    for r in range(tm // rc):
        rows = slice(r * rc, (r + 1) * rc)
        is_ctx = lax.broadcasted_iota(jnp.int32, (rc, 1), 0) + (r0 + r * rc) < ctx
        gate = jnp.where(is_ctx, g_ref[nb:nb + 1, :], g_ref[pl.ds(b, 1), :])
        o_ref[rows, :] = h_ref[rows, :] + gate * jnp.dot(x_ref[rows, :], w, preferred_element_type=F32)


def _residual_call(y, w, w_idx, h, mods, layer, g_gate, dims, name):
    m, k = y.shape
    d = h.shape[1]
    t, ctx, nb = dims
    tm, tn = _mm_tiles(t, k, d, 1, 8)
    nt = d // tn
    kern = functools.partial(_residual_kernel, tm=tm, tpb=t // tm, ctx=ctx, nb=nb)
    return pl.pallas_call(
        kern,
        out_shape=jax.ShapeDtypeStruct(h.shape, F32),
        grid=(m // tm, nt),
        in_specs=[pl.BlockSpec((tm, k), lambda i, j: (i, 0)),
                  pl.BlockSpec((None, k, tn), lambda i, j: (w_idx, 0, j)),
                  pl.BlockSpec((tm, tn), lambda i, j: (i, j)),
                  pl.BlockSpec((None, MOD_ROWS, tn), lambda i, j: (layer, 0, g_gate * nt + j))],
        out_specs=pl.BlockSpec((tm, tn), lambda i, j: (i, j)),
        input_output_aliases={2: 0},
        compiler_params=_params(("parallel", "arbitrary")),
        name=name,
    )(y, w, h, mods)


def _gated_out_kernel(of_ref, ob_ref, g_ref, nw_ref, w_ref, h_ref, m_ref, o_ref, *, dv, tm, rc, tpb, ctx, nb):
    i = pl.program_id(0)
    b = i // tpb
    r0 = (i % tpb) * tm
    nw = nw_ref[...]
    w = w_ref[...].astype(BF16)
    for r in range(tm // rc):
        rows = slice(r * rc, (r + 1) * rc)
        ys = []
        for hh in range(of_ref.shape[1] // dv):
            cols = slice(hh * dv, (hh + 1) * dv)
            o = of_ref[rows, cols] + ob_ref[rows, cols]
            y = o * lax.rsqrt(jnp.mean(o * o, axis=-1, keepdims=True) + EPS) * nw
            ys.append((y * _silu(g_ref[rows, cols])).astype(BF16))
        y = ys[0] if len(ys) == 1 else jnp.concatenate(ys, axis=1)
        is_ctx = lax.broadcasted_iota(jnp.int32, (rc, 1), 0) + (r0 + r * rc) < ctx
        gate = jnp.where(is_ctx, m_ref[nb:nb + 1, :], m_ref[pl.ds(b, 1), :])
        o_ref[rows, :] = h_ref[rows, :] + gate * jnp.dot(y, w, preferred_element_type=F32)


def _gated_out_call(o_f, o_b, gate_arr, gate_col0, o_norm, w_o, w_idx, h, mods, layer, g_gate, dims, name):
    m, width = o_f.shape
    d = h.shape[1]
    t, ctx, nb = dims
    dv = o_norm.shape[-1]
    tm = _row_tile(t, 272)
    while tm > 8 and 3 * 2 * tm * width * 4 + width * d * 2 + 4 * tm * d * 4 > 40 * 1024 * 1024:
        tm = _row_tile(t, tm - 8)
    rc = _row_tile(tm, 136)
    kern = functools.partial(_gated_out_kernel, dv=dv, tm=tm, rc=rc, tpb=t // tm, ctx=ctx, nb=nb)
    row_spec = pl.BlockSpec((tm, width), lambda i: (i, 0))
    return pl.pallas_call(
        kern,
        out_shape=jax.ShapeDtypeStruct(h.shape, F32),
        grid=(m // tm,),
        in_specs=[row_spec, row_spec,
                  pl.BlockSpec((tm, width), lambda i: (i, gate_col0 // width)),
                  pl.BlockSpec((None, 1, dv), lambda i: (w_idx, 0, 0)),
                  pl.BlockSpec((None, width, d), lambda i: (w_idx, 0, 0), pipeline_mode=pl.Buffered(1)),
                  pl.BlockSpec((tm, d), lambda i: (i, 0)),
                  pl.BlockSpec((None, MOD_ROWS, d), lambda i: (layer, 0, g_gate))],
        out_specs=pl.BlockSpec((tm, d), lambda i: (i, 0)),
        input_output_aliases={5: 0},
        compiler_params=_params(("parallel",)),
        name=name,
    )(o_f, o_b, gate_arr, o_norm.reshape(o_norm.shape[0], 1, dv), w_o, h, mods)


def _chunk_cumsum(g, c, reverse):
    n = g.shape[0]
    pos = jnp.bitwise_and(lax.broadcasted_iota(jnp.int32, (n, 1), 0), c - 1)
    s = 1
    while s < c:
        if reverse:
            g = g + jnp.where(pos < c - s, pltpu.roll(g, n - s, 0), 0.0)
        else:
            g = g + jnp.where(pos >= s, pltpu.roll(g, s, 0), 0.0)
        s *= 2
    return g


def _bwd_group(s, ng):
    return jnp.where(s == 0, 0, ng - s)


def _dot_nt(a, b):
    return lax.dot_general(a, b, (((1,), (1,)), ((), ())), preferred_element_type=F32)


def _gla_scan_kernel(qf_ref, kf_ref, vf_ref, gf_ref, qb_ref, kb_ref, vb_ref, gb_ref,
                     of_ref, ob_ref, st_ref, b_ref, *, gt, dk, dv, hpb):
    c = CHUNK
    nch = gt // c
    width = hpb * dk

    @pl.when(pl.program_id(2) == 0)
    def _():
        st_ref[...] = jnp.zeros_like(st_ref)

    crow = lax.broadcasted_iota(jnp.int32, (c, c), 0)
    ccol = lax.broadcasted_iota(jnp.int32, (c, c), 1)
    masks = (ccol <= crow, ccol >= crow)
    mid = (c // 2 - 1, c // 2)
    end = (c - 1, 0)
    g_refs = (gf_ref, gb_ref)
    q_refs = (qf_ref, qb_ref)
    k_refs = (kf_ref, kb_ref)
    v_refs = (vf_ref, vb_ref)
    o_refs = (of_ref, ob_ref)

    worst = jnp.zeros((1, width), F32)
    mag = jnp.ones((1, width), F32)
    for d in range(2):
        mag = jnp.maximum(mag, jnp.max(jnp.maximum(jnp.abs(q_refs[d][...]), jnp.abs(k_refs[d][...])),
                                       axis=0, keepdims=True))
        b = _chunk_cumsum(g_refs[d][...], c, reverse=d == 1)
        b_ref[d] = b
        for ci in range(nch):
            r = b[ci * c + mid[d]:ci * c + mid[d] + 1, :]
            e = b[ci * c + end[d]:ci * c + end[d] + 1, :]
            worst = jnp.maximum(worst, jnp.maximum(-r, r - e))
    safe = jnp.max(worst + jnp.log(mag)) <= EXP_RANGE

    def scores_fast(d, ci, q, k, b, kcols):
        r = b[mid[d]:mid[d] + 1, :]
        qs = (q * jnp.exp(b - r)).astype(BF16)
        ks = (k * jnp.exp(r - b)).astype(BF16)
        return jnp.where(masks[d], _dot_nt(qs, ks), 0.0)

    def scores_exact(d, ci, q, k, b, kcols):
        lane = lax.broadcasted_iota(jnp.int32, (c, c), 1)

        def body(s, acc):
            ks = k_refs[d][pl.ds(ci * c + s, 1), :][:, kcols]
            bs = b_ref[d, pl.ds(ci * c + s, 1), :][:, kcols]
            term = q * ks * jnp.exp(jnp.minimum(b - bs, 0.0))
            return jnp.where(lane == s, jnp.sum(term, axis=1, keepdims=True), acc)

        acc = lax.fori_loop(0, c, body, jnp.zeros((c, c), F32))
        return jnp.where(masks[d], acc, 0.0)

    def run(scores):
        pre = {}
        for hh in range(hpb):
            kcols = slice(hh * dk, (hh + 1) * dk)
            vcols = slice(hh * dv, (hh + 1) * dv)
            for d in range(2):
                for ci in range(nch):
                    rows = slice(ci * c, (ci + 1) * c)
                    q = q_refs[d][rows, kcols]
                    k = k_refs[d][rows, kcols]
                    v = v_refs[d][rows, vcols]
                    b = b_ref[d, rows, kcols]
                    b_end = b[end[d]:end[d] + 1, :]
                    a = scores(d, ci, q, k, b, kcols).astype(BF16)
                    ke = (k * jnp.exp(b_end - b)).astype(BF16)
                    pre[hh, d, ci] = dict(
                        av=jnp.dot(a, v.astype(BF16), preferred_element_type=F32),
                        qe=(q * jnp.exp(b)).astype(BF16),
                        upd=jnp.dot(v.T.astype(BF16), ke, preferred_element_type=F32),
                        dec=jnp.exp(b_end))
        for hh in range(hpb):
            vcols = slice(hh * dv, (hh + 1) * dv)
            for d in range(2):
                st = st_ref[d, hh]
                for step in range(nch):
                    ci = step if d == 0 else nch - 1 - step
                    p = pre[hh, d, ci]
                    o_refs[d][ci * c:(ci + 1) * c, vcols] = p["av"] + _dot_nt(p["qe"], st.astype(BF16))
                    st = st * p["dec"] + p["upd"]
                st_ref[d, hh] = st

    @pl.when(safe)
    def _():
        run(scores_fast)

    @pl.when(jnp.logical_not(safe))
    def _():
        run(scores_exact)


def _gla_scan_call(q_src, k_srcs, v_src, g_arr, nb, t, gt, heads, dk, dv):
    ng = t // gt
    hpb = 2 if heads % 2 == 0 else 1
    dkb, dvb = hpb * dk, hpb * dv

    def spec3(col0, width, bwd):
        off = col0 // width
        if bwd:
            return pl.BlockSpec((None, gt, width), lambda b, h, s: (b, _bwd_group(s, ng), off + h))
        return pl.BlockSpec((None, gt, width), lambda b, h, s: (b, s, off + h))

    def spec4(lead, col0, width, bwd):
        off = col0 // width
        if bwd:
            return pl.BlockSpec((None, None, gt, width),
                                lambda b, h, s: (lead, b, _bwd_group(s, ng), off + h))
        return pl.BlockSpec((None, None, gt, width), lambda b, h, s: (lead, b, s, off + h))

    def kspec(src, bwd):
        arr, lead, col0 = src
        return spec3(col0, dkb, bwd) if lead is None else spec4(lead, col0, dkb, bwd)

    in_specs, args = [], []
    for d in range(2):
        bwd = d == 1
        in_specs += [spec3(q_src[1], dkb, bwd), kspec(k_srcs[d], bwd), spec3(v_src[1], dvb, bwd),
                     spec4(d, 0, dkb, bwd)]
        args += [q_src[0], k_srcs[d][0], v_src[0], g_arr]
    out = jax.ShapeDtypeStruct((nb, t, heads * dv), F32)
    kern = functools.partial(_gla_scan_kernel, gt=gt, dk=dk, dv=dv, hpb=hpb)
    return pl.pallas_call(
        kern,
        out_shape=(out, out),
        grid=(nb, heads // hpb, ng),
        in_specs=in_specs,
        out_specs=(pl.BlockSpec((None, gt, dvb), lambda b, h, s: (b, s, h)),
                   pl.BlockSpec((None, gt, dvb), lambda b, h, s: (b, _bwd_group(s, ng), h))),
        scratch_shapes=[pltpu.VMEM((2, hpb, dv, dk), F32), pltpu.VMEM((2, gt, dkb), F32)],
        compiler_params=_params(("parallel", "parallel", "arbitrary")),
        name="gla_scan",
    )(*args)


def _dn_conv_kernel(x_ref, w_ref, o_ref, *, gt, n_norm_tiles, n_q_tiles, qscale):
    s = pl.program_id(1)
    j = pl.program_id(2)
    x = x_ref[...]
    w = w_ref[...]
    seg = jnp.where(s == 0, gt, GRID_W)
    r = lax.broadcasted_iota(jnp.int32, (gt, 1), 0)
    pos = jnp.where(s == 0, r, jnp.bitwise_and(r, GRID_W - 1))
    acc = x * w[CONV_K // 2:CONV_K // 2 + 1, :]
    for tap in range(CONV_K):
        off = tap - CONV_K // 2
        if off == 0:
            continue
        shifted = pltpu.roll(x, (-off) % gt, 0)
        ok = (pos + off >= 0) & (pos + off < seg)
        acc = acc + jnp.where(ok, shifted, 0.0) * w[tap:tap + 1, :]
    y = _silu(acc)
    o_ref[...] = y

    @pl.when(j < n_norm_tiles)
    def _():
        scale = jnp.where(j < n_q_tiles, qscale, 1.0)
        for hh in range(y.shape[1] // HEAD_DIM):
            cols = slice(hh * HEAD_DIM, (hh + 1) * HEAD_DIM)
            yh = y[:, cols]
            o_ref[:, cols] = yh * (lax.rsqrt(jnp.sum(yh * yh, axis=-1, keepdims=True) + EPS) * scale)


def _dn_conv_call(p, conv_w, layer_idx, nb, t, gt, qkv_w, key_w):
    tc = min(512, key_w)
    kern = functools.partial(_dn_conv_kernel, gt=gt, n_norm_tiles=2 * key_w // tc,
                             n_q_tiles=key_w // tc, qscale=HEAD_DIM ** -0.5)
    return pl.pallas_call(
        kern,
        out_shape=jax.ShapeDtypeStruct((nb, t, qkv_w), F32),
        grid=(nb, t // gt, qkv_w // tc),
        in_specs=[pl.BlockSpec((None, gt, tc), lambda b, s, j: (b, s, j)),
                  pl.BlockSpec((None, CONV_K, tc), lambda b, s, j: (layer_idx, 0, j))],
        out_specs=pl.BlockSpec((None, gt, tc), lambda b, s, j: (b, s, j)),
        compiler_params=_params(("parallel", "parallel", "parallel")),
        name="deltanet_conv",
    )(p, conv_w)


def _mm3(a, b):
    a1 = a.astype(BF16)
    a2 = (a - a1.astype(F32)).astype(BF16)
    b1 = b.astype(BF16)
    b2 = (b - b1.astype(F32)).astype(BF16)
    return (jnp.dot(a1, b1, preferred_element_type=F32) + jnp.dot(a1, b2, preferred_element_type=F32)
            + jnp.dot(a2, b1, preferred_element_type=F32))


def _mm1(a, b):
    return jnp.dot(a.astype(BF16), b.astype(BF16), preferred_element_type=F32)


INVERSE_EXACT_LEVELS = 1


def _neumann_inverses(ms, eye):
    c = CHUNK
    ps = [eye + m for m in ms]
    xs = [_mm3(m, m) for m in ms]
    levels = c.bit_length() - 2
    for lvl in range(levels):
        mm = _mm3 if lvl < INVERSE_EXACT_LEVELS else _mm1
        if lvl == levels - 1:
            ps = [p + mm(p, x) for p, x in zip(ps, xs)]
        else:
            rs = [mm(jnp.concatenate([x, p], axis=0), x) for p, x in zip(ps, xs)]
            xs = [r[:c] for r in rs]
            ps = [p + r[c:] for p, r in zip(ps, rs)]
    return ps


def _dn_scan_kernel(qf_ref, kf_ref, vf_ref, qb_ref, kb_ref, vb_ref, beta_f_ref, ld_f_ref,
                    beta_b_ref, ld_b_ref, of_ref, ob_ref, st_ref, *, gt, rep):
    c = CHUNK
    nch = gt // c

    @pl.when(pl.program_id(2) == 0)
    def _():
        st_ref[...] = jnp.zeros_like(st_ref)

    row = lax.broadcasted_iota(jnp.int32, (c, c), 0)
    col = lax.broadcasted_iota(jnp.int32, (c, c), 1)
    diag = row == col
    eye = jnp.where(diag, 1.0, 0.0)
    incl = (col <= row, col >= row)
    strict = (col < row, col > row)
    end = (c - 1, 0)
    q_refs = (qf_ref, qb_ref)
    k_refs = (kf_ref, kb_ref)
    v_refs = (vf_ref, vb_ref)
    o_refs = (of_ref, ob_ref)
    beta_refs = (beta_f_ref, beta_b_ref)
    ld_refs = (ld_f_ref, ld_b_ref)

    def to_col(x_row):
        return jnp.sum(jnp.where(diag, x_row, 0.0), axis=1, keepdims=True)

    units = {}
    for d in range(2):
        for ci in range(nch):
            rows = slice(ci * c, (ci + 1) * c)
            q = q_refs[d][rows, :]
            k = k_refs[d][rows, :]
            qb16 = q.astype(BF16)
            kb16 = k.astype(BF16)
            kk = _dot_nt(kb16, kb16)
            qk = _dot_nt(qb16, kb16)
            for r in range(rep):
                beta_row = beta_refs[d][r, ci:ci + 1, :]
                g_row = ld_refs[d][r, ci:ci + 1, :]
                g_col = to_col(g_row)
                beta_col = to_col(beta_row)
                gc_col = jnp.sum(jnp.where(incl[d], g_row, 0.0), axis=1, keepdims=True)
                gc_row = jnp.sum(jnp.where(incl[1 - d], g_col, 0.0), axis=0, keepdims=True)
                dec = jnp.exp(jnp.minimum(gc_col - gc_row, 0.0))
                gc_end = gc_col[end[d]:end[d] + 1, :]
                eg = jnp.exp(gc_col)
                v = v_refs[d][rows, r * HEAD_DIM:(r + 1) * HEAD_DIM]
                units[d, ci, r] = dict(
                    m=jnp.where(strict[d], -(beta_col * kk * dec), 0.0),
                    rhs_v=(beta_col * v).astype(BF16),
                    rhs_k=(beta_col * eg * k).astype(BF16),
                    a=jnp.where(incl[d], qk * dec, 0.0).astype(BF16),
                    kd_t=(k * jnp.exp(gc_end - gc_col)).T.astype(BF16),
                    eg=eg, dec_end=jnp.exp(gc_end), qb16=qb16)

    keys = list(units)
    for key, tinv in zip(keys, _neumann_inverses([units[key]["m"] for key in keys], eye)):
        u = units[key]
        sol = jnp.dot(tinv.astype(BF16), jnp.concatenate([u["rhs_v"], u["rhs_k"]], axis=1),
                      preferred_element_type=F32)
        u["sol_v"] = sol[:, :HEAD_DIM]
        u["kq"] = jnp.concatenate([sol[:, HEAD_DIM:].astype(BF16), u["qb16"]], axis=0)
        u["a_kd"] = jnp.concatenate([u["a"], u["kd_t"]], axis=0)

    chains = [(d, r) for d in range(2) for r in range(rep)]
    states = {ch: st_ref[ch[0], ch[1]] for ch in chains}
    for step in range(nch):
        for d, r in chains:
            ci = step if d == 0 else nch - 1 - step
            u = units[d, ci, r]
            st = states[d, r]
            ks_qs = jnp.dot(u["kq"], st.astype(BF16), preferred_element_type=F32)
            w16 = (u["sol_v"] - ks_qs[:c]).astype(BF16)
            aw_kw = jnp.dot(u["a_kd"], w16, preferred_element_type=F32)
            o_refs[d][ci * c:(ci + 1) * c, r * HEAD_DIM:(r + 1) * HEAD_DIM] = u["eg"] * ks_qs[c:] + aw_kw[:c]
            states[d, r] = st * u["dec_end"] + aw_kw[c:]
    for d, r in chains:
        st_ref[d, r] = states[d, r]


def _dn_scan_call(qkv, beta, ld, nb, t, gt, key_w, rep):
    ng = t // gt
    hd = HEAD_DIM
    hq = key_w // hd
    nch = gt // CHUNK
    koff = key_w // hd
    voff = 2 * key_w // (rep * hd)

    def spec(width, off, bwd):
        if bwd:
            return pl.BlockSpec((None, gt, width), lambda b, h, s: (b, _bwd_group(s, ng), off + h))
        return pl.BlockSpec((None, gt, width), lambda b, h, s: (b, s, off + h))

    def sspec(d):
        if d == 1:
            return pl.BlockSpec((None, None, rep, None, nch, CHUNK),
                                lambda b, h, s: (1, b, h, _bwd_group(s, ng), 0, 0))
        return pl.BlockSpec((None, None, rep, None, nch, CHUNK), lambda b, h, s: (0, b, h, s, 0, 0))

    in_specs, args = [], []
    for d in range(2):
        in_specs += [spec(hd, 0, d == 1), spec(hd, koff, d == 1), spec(rep * hd, voff, d == 1)]
        args += [qkv, qkv, qkv]
    in_specs += [sspec(0), sspec(0), sspec(1), sspec(1)]
    args += [beta, ld, beta, ld]
    out = jax.ShapeDtypeStruct((nb, t, hq * rep * hd), F32)
    kern = functools.partial(_dn_scan_kernel, gt=gt, rep=rep)
    return pl.pallas_call(
        kern,
        out_shape=(out, out),
        grid=(nb, hq, ng),
        in_specs=in_specs,
        out_specs=(pl.BlockSpec((None, gt, rep * hd), lambda b, h, s: (b, s, h)),
                   pl.BlockSpec((None, gt, rep * hd), lambda b, h, s: (b, _bwd_group(s, ng), h))),
        scratch_shapes=[pltpu.VMEM((2, rep, hd, hd), F32)],
        compiler_params=_params(("parallel", "parallel", "arbitrary")),
        name="deltanet_scan",
    )(*args)


def _final_norm_kernel(h_ref, w_ref, o_ref):
    x = h_ref[...]
    o_ref[...] = x * lax.rsqrt(jnp.mean(x * x, axis=-1, keepdims=True) + EPS) * w_ref[...]


def _final_norm_call(h3, w, ctx, seq):
    nb, t, d = h3.shape
    skip = ctx // ctx
    return pl.pallas_call(
        _final_norm_kernel,
        out_shape=jax.ShapeDtypeStruct((nb, seq, d), F32),
        grid=(nb, seq // ctx),
        in_specs=[pl.BlockSpec((None, ctx, d), lambda b, i: (b, i + skip, 0)),
                  pl.BlockSpec((1, d), lambda b, i: (0, 0))],
        out_specs=pl.BlockSpec((None, ctx, d), lambda b, i: (b, i, 0)),
        compiler_params=_params(("parallel", "parallel")),
        name="final_norm",
    )(h3, w.reshape(1, d))


def _scan_scalars(x, nb, t, gt):
    hv = x.shape[-1]
    x = x.reshape(2, nb, t, hv).transpose(0, 1, 3, 2)
    return x.reshape(2, nb, hv, t // gt, gt // CHUNK, CHUNK)


def kernel(x, c, ctx, c_ctx, w_ada, b_ada, norm_mix, norm_ffn, hg_lb_logits, hg_w_in, hg_w_f, hg_o_norm, hg_w_o, gla_w_in, gla_w_g1, gla_w_g2, gla_b_g, gla_o_norm, gla_w_o, dn_w_in, dn_conv, dn_w_b, dn_w_a, dn_a_log, dn_dt_bias, dn_o_norm, dn_w_o, ffn_w_gate, ffn_w_up, ffn_w_down, final_norm):
    nb, seq, d = x.shape
    n_ctx = ctx.shape[1]
    depth = w_ada.shape[0]
    t = n_ctx + seq
    gt = n_ctx
    assert n_ctx % CHUNK == 0 and seq % gt == 0 and GRID_W == CHUNK and nb + 1 <= MOD_ROWS
    dims = (t, n_ctx, nb)
    m = nb * t

    (hg_w_in, hg_w_f, hg_w_o, gla_w_in, gla_w_g1, gla_w_g2, gla_w_o, dn_w_in, dn_w_b, dn_w_a, dn_w_o,
     ffn_w_gate, ffn_w_up, ffn_w_down) = (
        w.astype(BF16) for w in (hg_w_in, hg_w_f, hg_w_o, gla_w_in, gla_w_g1, gla_w_g2, gla_w_o, dn_w_in,
                                 dn_w_b, dn_w_a, dn_w_o, ffn_w_gate, ffn_w_up, ffn_w_down))

    p = jax.nn.softmax(hg_lb_logits.astype(F32), axis=0)
    lower_bounds = jnp.cumsum(p, axis=0) - p[0]

    cc = jnp.zeros((MOD_ROWS, d), F32).at[:nb].set(c).at[nb].set(c_ctx)
    mods = _ada_call(cc, w_ada, b_ada)

    h = jnp.concatenate([ctx, x], axis=1).reshape(m, d)

    hg_heads = d // HEAD_DIM
    hg_kw = hg_w_f.shape[-1]
    gla_kw = gla_w_g2.shape[-1]
    gla_dk = gla_kw // GLA_HEADS
    gla_dv = d // GLA_HEADS
    dn_key_w = d
    dn_hv = dn_w_b.shape[-1]
    dn_val_w = dn_hv * HEAD_DIM
    dn_qkv_w = 2 * dn_key_w + dn_val_w
    rep = dn_hv // (dn_key_w // HEAD_DIM)

    for i in range(depth):
        kind, j = i % 3, i // 3
        src = _NormSrc(h, norm_mix.reshape(depth, 1, d), mods, i, 0, 1, dims)
        if kind == 0:
            proj = _proj_call(lambda acc: _silu(acc) * HEAD_DIM ** -0.5, src, hg_w_in, j,
                              "hgrn2_in_proj", hg_kw)
            kk, lf = _hg_f_call(src, hg_w_f, j, lower_bounds[i])
            p3 = proj.reshape(nb, t, -1)
            kk4 = kk.reshape(2, nb, t, hg_kw)
            o_f, o_b = _gla_scan_call((p3, 0), ((kk4, 0, 0), (kk4, 1, 0)), (p3, hg_kw),
                                      lf.reshape(2, nb, t, hg_kw), nb, t, gt, hg_heads, HEAD_DIM,
                                      d // hg_heads)
            h = _gated_out_call(o_f.reshape(m, -1), o_b.reshape(m, -1), proj, hg_kw + d, hg_o_norm, hg_w_o, j,
                                h, mods, i, 2, dims, "hgrn2_out_proj")
        elif kind == 1:
            proj = _proj_call(lambda acc: acc * gla_dk ** -0.5, src, gla_w_in, j,
                              "gla_in_proj", gla_kw)
            lg = _gla_gate_call(src, gla_w_g1, gla_w_g2, gla_b_g, j)
            p3 = proj.reshape(nb, t, -1)
            o_f, o_b = _gla_scan_call((p3, 0), ((p3, None, gla_kw), (p3, None, gla_kw)), (p3, 2 * gla_kw),
                                      lg.reshape(2, nb, t, gla_kw), nb, t, gt, GLA_HEADS, gla_dk, gla_dv)
            h = _gated_out_call(o_f.reshape(m, -1), o_b.reshape(m, -1), proj, 2 * gla_kw + d, gla_o_norm,
                                gla_w_o, j, h, mods, i, 2, dims, "gla_out_proj")
        else:
            proj = _proj_call(None, src, dn_w_in, j, "deltanet_in_proj")
            beta, ld = _dn_bg_call(src, dn_w_b, dn_w_a, dn_a_log, dn_dt_bias, j)
            qkv = _dn_conv_call(proj.reshape(nb, t, -1), dn_conv, j, nb, t, gt, dn_qkv_w, dn_key_w)
            o_f, o_b = _dn_scan_call(qkv, _scan_scalars(beta, nb, t, gt), _scan_scalars(ld, nb, t, gt),
                                     nb, t, gt, dn_key_w, rep)
            h = _gated_out_call(o_f.reshape(m, -1), o_b.reshape(m, -1), proj, dn_qkv_w, dn_o_norm, dn_w_o, j,
                                h, mods, i, 2, dims, "deltanet_out_proj")
        src = _NormSrc(h, norm_ffn.reshape(depth, 1, d), mods, i, 3, 4, dims)
        u = _swiglu_call(src, ffn_w_gate, ffn_w_up, i)
        h = _residual_call(u, ffn_w_down, i, h, mods, i, 5, dims, "ffn_down_proj")

    return _final_norm_call(h.reshape(nb, t, d), final_norm, n_ctx, seq)
```

```python
import functools
from typing import NamedTuple

import jax
import jax.numpy as jnp
from jax import lax
from jax.experimental import pallas as pl
from jax.experimental.pallas import tpu as pltpu

F32 = jnp.float32
BF16 = jnp.bfloat16

EPS = 1e-6
GATE_CLIP = 30.0
CHUNK = 64
GRID_W = 64
CONV_K = 5
GLA_HEADS = 4
GLA_GATE_NORM = 16.0
HEAD_DIM = 128
EXP_RANGE = 80.0

V7X_VMEM_LIMIT = 56 * 1024 * 1024
VMEM_BLOCK_BUDGET = 44 * 1024 * 1024
BF16_SUBLANES = 16
MOD_ROWS = 8


def _params(sem):
    return pltpu.CompilerParams(dimension_semantics=sem, vmem_limit_bytes=V7X_VMEM_LIMIT)


def _sigmoid(x):
    return 1.0 / (1.0 + jnp.exp(-x))


def _silu(x):
    return x * _sigmoid(x)


def _softplus(x):
    return jnp.maximum(x, 0.0) + jnp.log1p(jnp.exp(-jnp.abs(x)))


def _log_sigmoid(x):
    return -_softplus(-x)


def _row_tile(t, target):
    best = 8
    for d in range(8, min(t, target) + 1, 8):
        if t % d == 0:
            best = d
    return best


def _col_tile(n, target):
    for c in (target, 512, 256, 128):
        if c <= target and n % c == 0:
            return c
    return n


def _mod_rows(m_ref, i, tm, tiles_per_batch, ctx, nb):
    b = i // tiles_per_batch
    r0 = (i % tiles_per_batch) * tm
    lat = m_ref[pl.ds(b, 1), :]
    cx = m_ref[nb:nb + 1, :]
    rows = lax.broadcasted_iota(jnp.int32, (tm, 1), 0) + r0
    return jnp.where(rows < ctx, cx, lat)


def _ada_kernel(c_ref, w_ref, b_ref, o_ref):
    s = _silu(c_ref[...]).astype(BF16)
    o_ref[...] = jnp.dot(s, w_ref[...].astype(BF16), preferred_element_type=F32) + b_ref[...]


def _ada_call(cc, w_ada, b_ada):
    depth, d, n = w_ada.shape
    tn = _col_tile(n, 1024)
    return pl.pallas_call(
        _ada_kernel,
        out_shape=jax.ShapeDtypeStruct((depth, MOD_ROWS, n), F32),
        grid=(depth, n // tn),
        in_specs=[pl.BlockSpec((MOD_ROWS, d), lambda i, j: (0, 0)),
                  pl.BlockSpec((None, d, tn), lambda i, j: (i, 0, j)),
                  pl.BlockSpec((None, 1, tn), lambda i, j: (i, 0, j))],
        out_specs=pl.BlockSpec((None, MOD_ROWS, tn), lambda i, j: (i, 0, j)),
        compiler_params=_params(("parallel", "parallel")),
        name="ada_mods",
    )(cc, w_ada, b_ada.reshape(depth, 1, n))


class _NormSrc(NamedTuple):
    h: jax.Array
    norm_w: jax.Array
    mods: jax.Array
    layer: int
    g_shift: int
    g_scale: int
    dims: tuple


def _dot(x, w_ref):
    return jnp.dot(x, w_ref[...].astype(BF16), preferred_element_type=F32)


def _row_chunk(tm):
    best = tm
    for rc in range(BF16_SUBLANES, min(tm, 272) + 1, BF16_SUBLANES):
        if tm % rc == 0:
            best = rc
    return best


def _fused(epilogue, src, tm, n_w):
    t, ctx, nb = src.dims
    tpb = t // tm
    rc = _row_chunk(tm)

    def kern(h_ref, nw_ref, sh_ref, sc_ref, *rest):
        a_ref = rest[-1]
        w_refs = rest[:n_w]
        n_out = epilogue.n_out
        par_refs = rest[n_w:len(rest) - 1 - n_out]
        out_refs = rest[len(rest) - 1 - n_out:-1]
        i = pl.program_id(0)
        b = i // tpb
        r0 = (i % tpb) * tm

        def compute(first):
            ws = [w[...].astype(BF16) for w in w_refs]
            for r in range(tm // rc):
                rows = slice(r * rc, (r + 1) * rc)
                if first:
                    x = h_ref[rows, :]
                    y = x * lax.rsqrt(jnp.mean(x * x, axis=-1, keepdims=True) + EPS) * nw_ref[...]
                    is_ctx = lax.broadcasted_iota(jnp.int32, (rc, 1), 0) + (r0 + r * rc) < ctx
                    scale = jnp.where(is_ctx, sc_ref[nb:nb + 1, :], sc_ref[pl.ds(b, 1), :])
                    shift = jnp.where(is_ctx, sh_ref[nb:nb + 1, :], sh_ref[pl.ds(b, 1), :])
                    a = (y * (1.0 + scale) + shift).astype(BF16)
                    a_ref[rows, :] = a
                else:
                    a = a_ref[rows, :]
                accs = [jnp.dot(a, w, preferred_element_type=F32) for w in ws]
                outs = epilogue(accs, par_refs)
                for o_ref, o in zip(out_refs, outs):
                    o_ref[rows, :] = o.astype(o_ref.dtype)

        @pl.when(pl.program_id(1) == 0)
        def _():
            compute(True)

        @pl.when(pl.program_id(1) > 0)
        def _():
            compute(False)

    return kern


def _src_specs(src, tm):
    d = src.h.shape[1]
    specs = [pl.BlockSpec((tm, d), lambda i, j: (i, 0), pipeline_mode=pl.Buffered(1)),
             pl.BlockSpec((None, 1, d), lambda i, j: (src.layer, 0, 0)),
             pl.BlockSpec((None, MOD_ROWS, d), lambda i, j: (src.layer, 0, src.g_shift)),
             pl.BlockSpec((None, MOD_ROWS, d), lambda i, j: (src.layer, 0, src.g_scale))]
    return specs, [src.h, src.norm_w, src.mods, src.mods]


class _Epilogue:
    def __init__(self, fn, n_out=1):
        self.fn = fn
        self.n_out = n_out

    def __call__(self, accs, par_refs):
        outs = self.fn(accs, par_refs)
        return outs if isinstance(outs, (tuple, list)) else (outs,)


def _fused_call(epilogue, src, tm, grid_inner, n_w, in_specs, args, out_shape, out_specs, name):
    m, d = src.h.shape
    specs, src_args = _src_specs(src, tm)
    return pl.pallas_call(
        _fused(epilogue, src, tm, n_w),
        out_shape=out_shape,
        grid=(m // tm, grid_inner),
        in_specs=specs + in_specs,
        out_specs=out_specs,
        scratch_shapes=[pltpu.VMEM((tm, d), BF16)],
        compiler_params=_params(("parallel", "arbitrary")),
        name=name,
    )(*src_args, *args)


def _mm_tiles(t, k, n, n_w, out_bytes, x_bytes=4, fixed=0, tm_target=1088, tn_target=512):
    tm = _row_tile(t, tm_target)
    while tm > 64:
        for tn in sorted({tn_target, 512, 256}, reverse=True):
            if tn > tn_target:
                continue
            need = tm * k * x_bytes + fixed + 2 * n_w * k * tn * 2 + 2 * tm * tn * out_bytes
            if n % tn == 0 and need <= VMEM_BLOCK_BUDGET:
                return tm, tn
        tm = _row_tile(t, tm - 8)
    return tm, _col_tile(n, 128)


def _fused_tiles(t, k, n, n_w, out_bytes, tn_target=1024):
    return _mm_tiles(t, k, n, n_w, out_bytes, x_bytes=6, fixed=8 * 1024 * 1024, tn_target=tn_target)


def _proj_call(epi_q, src, w, layer_idx, name, q_cols=0):
    m, k = src.h.shape
    n = w.shape[-1]
    tm, tn = _fused_tiles(src.dims[0], k, n, 1, 4)
    while q_cols % tn:
        tn //= 2

    def epi(accs, par_refs):
        acc = accs[0]
        if epi_q is None:
            return acc
        return jnp.where(pl.program_id(1) * tn < q_cols, epi_q(acc), acc)

    return _fused_call(
        _Epilogue(epi), src, tm, n // tn, 1,
        [pl.BlockSpec((None, k, tn), lambda i, j: (layer_idx, 0, j))], [w],
        jax.ShapeDtypeStruct((m, n), F32), pl.BlockSpec((tm, tn), lambda i, j: (i, j)), name)


def _hg_f_epilogue(accs, par_refs):
    z = jnp.clip(accs[0], -GATE_CLIP, GATE_CLIP)
    lb = par_refs[0][...]
    u = jnp.exp(-z)
    r = 1.0 / (1.0 + u)
    return (1.0 - lb) * (u * r), jnp.log((1.0 + lb * u) * r)


def _hg_f_call(src, w_f, layer_idx, lb):
    m, k = src.h.shape
    n = w_f.shape[-1]
    tm, tn = _fused_tiles(src.dims[0], k, n, 1, 8)
    nt = n // tn
    out = jax.ShapeDtypeStruct((2, m, n), F32)
    ospec = pl.BlockSpec((None, tm, tn), lambda i, j: (j // nt, i, j % nt))
    return _fused_call(
        _Epilogue(_hg_f_epilogue, 2), src, tm, 2 * nt, 1,
        [pl.BlockSpec((None, None, k, tn), lambda i, j: (layer_idx, j // nt, 0, j % nt)),
         pl.BlockSpec((1, tn), lambda i, j: (0, j % nt))],
        [w_f, lb.reshape(1, n)], (out, out), (ospec, ospec), "hgrn2_gates")


def _gla_gate_epilogue(accs, par_refs):
    w2_ref, b_ref = par_refs
    gl = jnp.dot(accs[0].astype(BF16), w2_ref[...].astype(BF16), preferred_element_type=F32) + b_ref[...]
    return _log_sigmoid(gl) * (1.0 / GLA_GATE_NORM)


def _gla_gate_call(src, w_g1, w_g2, b_g, layer_idx):
    m, k = src.h.shape
    rank, kw = w_g2.shape[-2:]
    tm = _row_tile(src.dims[0], 544)
    return _fused_call(
        _Epilogue(_gla_gate_epilogue), src, tm, 2, 1,
        [pl.BlockSpec((None, None, k, rank), lambda i, d: (layer_idx, d, 0, 0)),
         pl.BlockSpec((None, None, rank, kw), lambda i, d: (layer_idx, d, 0, 0)),
         pl.BlockSpec((None, None, 1, kw), lambda i, d: (layer_idx, d, 0, 0))],
        [w_g1, w_g2, b_g.reshape(b_g.shape[0], 2, 1, kw)],
        jax.ShapeDtypeStruct((2, m, kw), F32), pl.BlockSpec((None, tm, kw), lambda i, d: (d, i, 0)),
        "gla_gates")


def _dn_bg_epilogue(accs, par_refs):
    alog_ref, dtb_ref = par_refs
    beta = _sigmoid(accs[0])
    dt = _softplus(accs[1] + dtb_ref[...])
    return beta, -jnp.exp(alog_ref[...]) * dt


def _dn_bg_call(src, w_b, w_a, a_log, dt_bias, layer_idx):
    m, k = src.h.shape
    hv = w_b.shape[-1]
    tm = _row_tile(src.dims[0], 544)
    out = jax.ShapeDtypeStruct((2, m, hv), F32)
    wspec = pl.BlockSpec((None, None, k, hv), lambda i, d: (layer_idx, d, 0, 0))
    pspec = pl.BlockSpec((None, None, 1, hv), lambda i, d: (layer_idx, d, 0, 0))
    ospec = pl.BlockSpec((None, tm, hv), lambda i, d: (d, i, 0))
    nl = a_log.shape[0]
    return _fused_call(
        _Epilogue(_dn_bg_epilogue, 2), src, tm, 2, 2, [wspec, wspec, pspec, pspec],
        [w_b, w_a, a_log.reshape(nl, 2, 1, hv), dt_bias.reshape(nl, 2, 1, hv)],
        (out, out), (ospec, ospec), "deltanet_beta_decay")


def _swiglu_epilogue(accs, par_refs):
    return _silu(accs[0]) * accs[1]


def _swiglu_call(src, w_gate, w_up, layer):
    m, k = src.h.shape
    n = w_gate.shape[-1]
    tm, tn = _fused_tiles(src.dims[0], k, n, 2, 2, tn_target=512)
    wspec = pl.BlockSpec((None, k, tn), lambda i, j: (layer, 0, j))
    return _fused_call(
        _Epilogue(_swiglu_epilogue), src, tm, n // tn, 2, [wspec, wspec], [w_gate, w_up],
        jax.ShapeDtypeStruct((m, n), BF16), pl.BlockSpec((tm, tn), lambda i, j: (i, j)), "ffn_swiglu")


def _residual_kernel(x_ref, w_ref, h_ref, g_ref, o_ref, *, tm, tpb, ctx, nb):
    i = pl.program_id(0)
    b = i // tpb
    r0 = (i % tpb) * tm
    rc = _row_chunk(tm)
    w = w_ref[...].astype(BF16)
    for r in range(tm // rc):
        rows = slice(r * rc, (r + 1) * rc)
        is_ctx = lax.broadcasted_iota(jnp.int32, (rc, 1), 0) + (r0 + r * rc) < ctx
        gate = jnp.where(is_ctx, g_ref[nb:nb + 1, :], g_ref[pl.ds(b, 1), :])
        o_ref[rows, :] = h_ref[rows, :] + gate * jnp.dot(x_ref[rows, :], w, preferred_element_type=F32)


def _residual_call(y, w, w_idx, h, mods, layer, g_gate, dims, name):
    m, k = y.shape
    d = h.shape[1]
    t, ctx, nb = dims
    tm, tn = _mm_tiles(t, k, d, 1, 8)
    nt = d // tn
    kern = functools.partial(_residual_kernel, tm=tm, tpb=t // tm, ctx=ctx, nb=nb)
    return pl.pallas_call(
        kern,
        out_shape=jax.ShapeDtypeStruct(h.shape, F32),
        grid=(m // tm, nt),
        in_specs=[pl.BlockSpec((tm, k), lambda i, j: (i, 0)),
                  pl.BlockSpec((None, k, tn), lambda i, j: (w_idx, 0, j)),
                  pl.BlockSpec((tm, tn), lambda i, j: (i, j)),
                  pl.BlockSpec((None, MOD_ROWS, tn), lambda i, j: (layer, 0, g_gate * nt + j))],
        out_specs=pl.BlockSpec((tm, tn), lambda i, j: (i, j)),
        input_output_aliases={2: 0},
        compiler_params=_params(("parallel", "arbitrary")),
        name=name,
    )(y, w, h, mods)


def _gated_out_kernel(of_ref, ob_ref, g_ref, nw_ref, w_ref, h_ref, m_ref, o_ref, *, dv, tm, rc, tpb, ctx, nb):
    i = pl.program_id(0)
    b = i // tpb
    r0 = (i % tpb) * tm
    nw = nw_ref[...]
    w = w_ref[...].astype(BF16)
    for r in range(tm // rc):
        rows = slice(r * rc, (r + 1) * rc)
        ys = []
        for hh in range(of_ref.shape[1] // dv):
            cols = slice(hh * dv, (hh + 1) * dv)
            o = of_ref[rows, cols] + ob_ref[rows, cols]
            y = o * lax.rsqrt(jnp.mean(o * o, axis=-1, keepdims=True) + EPS) * nw
            ys.append((y * _silu(g_ref[rows, cols])).astype(BF16))
        y = ys[0] if len(ys) == 1 else jnp.concatenate(ys, axis=1)
        is_ctx = lax.broadcasted_iota(jnp.int32, (rc, 1), 0) + (r0 + r * rc) < ctx
        gate = jnp.where(is_ctx, m_ref[nb:nb + 1, :], m_ref[pl.ds(b, 1), :])
        o_ref[rows, :] = h_ref[rows, :] + gate * jnp.dot(y, w, preferred_element_type=F32)


def _gated_out_call(o_f, o_b, gate_arr, gate_col0, o_norm, w_o, w_idx, h, mods, layer, g_gate, dims, name):
    m, width = o_f.shape
    d = h.shape[1]
    t, ctx, nb = dims
    dv = o_norm.shape[-1]
    tm = _row_tile(t, 272)
    while tm > 8 and 3 * 2 * tm * width * 4 + width * d * 2 + 4 * tm * d * 4 > 40 * 1024 * 1024:
        tm = _row_tile(t, tm - 8)
    rc = _row_tile(tm, 136)
    kern = functools.partial(_gated_out_kernel, dv=dv, tm=tm, rc=rc, tpb=t // tm, ctx=ctx, nb=nb)
    row_spec = pl.BlockSpec((tm, width), lambda i: (i, 0))
    return pl.pallas_call(
        kern,
        out_shape=jax.ShapeDtypeStruct(h.shape, F32),
        grid=(m // tm,),
        in_specs=[row_spec, row_spec,
                  pl.BlockSpec((tm, width), lambda i: (i, gate_col0 // width)),
                  pl.BlockSpec((None, 1, dv), lambda i: (w_idx, 0, 0)),
                  pl.BlockSpec((None, width, d), lambda i: (w_idx, 0, 0), pipeline_mode=pl.Buffered(1)),
                  pl.BlockSpec((tm, d), lambda i: (i, 0)),
                  pl.BlockSpec((None, MOD_ROWS, d), lambda i: (layer, 0, g_gate))],
        out_specs=pl.BlockSpec((tm, d), lambda i: (i, 0)),
        input_output_aliases={5: 0},
        compiler_params=_params(("parallel",)),
        name=name,
    )(o_f, o_b, gate_arr, o_norm.reshape(o_norm.shape[0], 1, dv), w_o, h, mods)


def _chunk_cumsum(g, c, reverse):
    n = g.shape[0]
    pos = jnp.bitwise_and(lax.broadcasted_iota(jnp.int32, (n, 1), 0), c - 1)
    s = 1
    while s < c:
        if reverse:
            g = g + jnp.where(pos < c - s, pltpu.roll(g, n - s, 0), 0.0)
        else:
            g = g + jnp.where(pos >= s, pltpu.roll(g, s, 0), 0.0)
        s *= 2
    return g


def _bwd_group(s, ng):
    return jnp.where(s == 0, 0, ng - s)


def _dot_nt(a, b):
    return lax.dot_general(a, b, (((1,), (1,)), ((), ())), preferred_element_type=F32)


def _gla_scan_kernel(qf_ref, kf_ref, vf_ref, gf_ref, qb_ref, kb_ref, vb_ref, gb_ref,
                     of_ref, ob_ref, st_ref, b_ref, *, gt, dk, dv, hpb):
    c = CHUNK
    nch = gt // c
    width = hpb * dk

    @pl.when(pl.program_id(2) == 0)
    def _():
        st_ref[...] = jnp.zeros_like(st_ref)

    crow = lax.broadcasted_iota(jnp.int32, (c, c), 0)
    ccol = lax.broadcasted_iota(jnp.int32, (c, c), 1)
    masks = (ccol <= crow, ccol >= crow)
    mid = (c // 2 - 1, c // 2)
    end = (c - 1, 0)
    g_refs = (gf_ref, gb_ref)
    q_refs = (qf_ref, qb_ref)
    k_refs = (kf_ref, kb_ref)
    v_refs = (vf_ref, vb_ref)
    o_refs = (of_ref, ob_ref)

    worst = jnp.zeros((1, width), F32)
    mag = jnp.ones((1, width), F32)
    for d in range(2):
        mag = jnp.maximum(mag, jnp.max(jnp.maximum(jnp.abs(q_refs[d][...]), jnp.abs(k_refs[d][...])),
                                       axis=0, keepdims=True))
        b = _chunk_cumsum(g_refs[d][...], c, reverse=d == 1)
        b_ref[d] = b
        for ci in range(nch):
            r = b[ci * c + mid[d]:ci * c + mid[d] + 1, :]
            e = b[ci * c + end[d]:ci * c + end[d] + 1, :]
            worst = jnp.maximum(worst, jnp.maximum(-r, r - e))
    safe = jnp.max(worst + jnp.log(mag)) <= EXP_RANGE

    def scores_fast(d, ci, q, k, b, kcols):
        r = b[mid[d]:mid[d] + 1, :]
        qs = (q * jnp.exp(b - r)).astype(BF16)
        ks = (k * jnp.exp(r - b)).astype(BF16)
        return jnp.where(masks[d], _dot_nt(qs, ks), 0.0)

    def scores_exact(d, ci, q, k, b, kcols):
        lane = lax.broadcasted_iota(jnp.int32, (c, c), 1)

        def body(s, acc):
            ks = k_refs[d][pl.ds(ci * c + s, 1), :][:, kcols]
            bs = b_ref[d, pl.ds(ci * c + s, 1), :][:, kcols]
            term = q * ks * jnp.exp(jnp.minimum(b - bs, 0.0))
            return jnp.where(lane == s, jnp.sum(term, axis=1, keepdims=True), acc)

        acc = lax.fori_loop(0, c, body, jnp.zeros((c, c), F32))
        return jnp.where(masks[d], acc, 0.0)

    def run(scores):
        pre = {}
        for hh in range(hpb):
            kcols = slice(hh * dk, (hh + 1) * dk)
            vcols = slice(hh * dv, (hh + 1) * dv)
            for d in range(2):
                for ci in range(nch):
                    rows = slice(ci * c, (ci + 1) * c)
                    q = q_refs[d][rows, kcols]
                    k = k_refs[d][rows, kcols]
                    v = v_refs[d][rows, vcols]
                    b = b_ref[d, rows, kcols]
                    b_end = b[end[d]:end[d] + 1, :]
                    a = scores(d, ci, q, k, b, kcols).astype(BF16)
                    ke = (k * jnp.exp(b_end - b)).astype(BF16)
                    pre[hh, d, ci] = dict(
                        av=jnp.dot(a, v.astype(BF16), preferred_element_type=F32),
                        qe=(q * jnp.exp(b)).astype(BF16),
                        upd=jnp.dot(v.T.astype(BF16), ke, preferred_element_type=F32),
                        dec=jnp.exp(b_end))
        for hh in range(hpb):
            vcols = slice(hh * dv, (hh + 1) * dv)
            for d in range(2):
                st = st_ref[d, hh]
                for step in range(nch):
                    ci = step if d == 0 else nch - 1 - step
                    p = pre[hh, d, ci]
                    o_refs[d][ci * c:(ci + 1) * c, vcols] = p["av"] + _dot_nt(p["qe"], st.astype(BF16))
                    st = st * p["dec"] + p["upd"]
                st_ref[d, hh] = st

    @pl.when(safe)
    def _():
        run(scores_fast)

    @pl.when(jnp.logical_not(safe))
    def _():
        run(scores_exact)


def _gla_scan_call(q_src, k_srcs, v_src, g_arr, nb, t, gt, heads, dk, dv):
    ng = t // gt
    hpb = 2 if heads % 2 == 0 else 1
    dkb, dvb = hpb * dk, hpb * dv

    def spec3(col0, width, bwd):
        off = col0 // width
        if bwd:
            return pl.BlockSpec((None, gt, width), lambda b, h, s: (b, _bwd_group(s, ng), off + h))
        return pl.BlockSpec((None, gt, width), lambda b, h, s: (b, s, off + h))

    def spec4(lead, col0, width, bwd):
        off = col0 // width
        if bwd:
            return pl.BlockSpec((None, None, gt, width),
                                lambda b, h, s: (lead, b, _bwd_group(s, ng), off + h))
        return pl.BlockSpec((None, None, gt, width), lambda b, h, s: (lead, b, s, off + h))

    def kspec(src, bwd):
        arr, lead, col0 = src
        return spec3(col0, dkb, bwd) if lead is None else spec4(lead, col0, dkb, bwd)

    in_specs, args = [], []
    for d in range(2):
        bwd = d == 1
        in_specs += [spec3(q_src[1], dkb, bwd), kspec(k_srcs[d], bwd), spec3(v_src[1], dvb, bwd),
                     spec4(d, 0, dkb, bwd)]
        args += [q_src[0], k_srcs[d][0], v_src[0], g_arr]
    out = jax.ShapeDtypeStruct((nb, t, heads * dv), F32)
    kern = functools.partial(_gla_scan_kernel, gt=gt, dk=dk, dv=dv, hpb=hpb)
    return pl.pallas_call(
        kern,
        out_shape=(out, out),
        grid=(nb, heads // hpb, ng),
        in_specs=in_specs,
        out_specs=(pl.BlockSpec((None, gt, dvb), lambda b, h, s: (b, s, h)),
                   pl.BlockSpec((None, gt, dvb), lambda b, h, s: (b, _bwd_group(s, ng), h))),
        scratch_shapes=[pltpu.VMEM((2, hpb, dv, dk), F32), pltpu.VMEM((2, gt, dkb), F32)],
        compiler_params=_params(("parallel", "parallel", "arbitrary")),
        name="gla_scan",
    )(*args)


def _dn_conv_kernel(x_ref, w_ref, o_ref, *, gt, n_norm_tiles, n_q_tiles, qscale):
    s = pl.program_id(1)
    j = pl.program_id(2)
    x = x_ref[...]
    w = w_ref[...]
    seg = jnp.where(s == 0, gt, GRID_W)
    r = lax.broadcasted_iota(jnp.int32, (gt, 1), 0)
    pos = jnp.where(s == 0, r, jnp.bitwise_and(r, GRID_W - 1))
    acc = x * w[CONV_K // 2:CONV_K // 2 + 1, :]
    for tap in range(CONV_K):
        off = tap - CONV_K // 2
        if off == 0:
            continue
        shifted = pltpu.roll(x, (-off) % gt, 0)
        ok = (pos + off >= 0) & (pos + off < seg)
        acc = acc + jnp.where(ok, shifted, 0.0) * w[tap:tap + 1, :]
    y = _silu(acc)
    o_ref[...] = y

    @pl.when(j < n_norm_tiles)
    def _():
        scale = jnp.where(j < n_q_tiles, qscale, 1.0)
        for hh in range(y.shape[1] // HEAD_DIM):
            cols = slice(hh * HEAD_DIM, (hh + 1) * HEAD_DIM)
            yh = y[:, cols]
            o_ref[:, cols] = yh * (lax.rsqrt(jnp.sum(yh * yh, axis=-1, keepdims=True) + EPS) * scale)


def _dn_conv_call(p, conv_w, layer_idx, nb, t, gt, qkv_w, key_w):
    tc = min(2048, key_w)
    kern = functools.partial(_dn_conv_kernel, gt=gt, n_norm_tiles=2 * key_w // tc,
                             n_q_tiles=key_w // tc, qscale=HEAD_DIM ** -0.5)
    return pl.pallas_call(
        kern,
        out_shape=jax.ShapeDtypeStruct((nb, t, qkv_w), F32),
        grid=(nb, t // gt, qkv_w // tc),
        in_specs=[pl.BlockSpec((None, gt, tc), lambda b, s, j: (b, s, j)),
                  pl.BlockSpec((None, CONV_K, tc), lambda b, s, j: (layer_idx, 0, j))],
        out_specs=pl.BlockSpec((None, gt, tc), lambda b, s, j: (b, s, j)),
        compiler_params=_params(("parallel", "parallel", "parallel")),
        name="deltanet_conv",
    )(p, conv_w)


def _mm3(a, b):
    a1 = a.astype(BF16)
    a2 = (a - a1.astype(F32)).astype(BF16)
    b1 = b.astype(BF16)
    b2 = (b - b1.astype(F32)).astype(BF16)
    return (jnp.dot(a1, b1, preferred_element_type=F32) + jnp.dot(a1, b2, preferred_element_type=F32)
            + jnp.dot(a2, b1, preferred_element_type=F32))


def _mm1(a, b):
    return jnp.dot(a.astype(BF16), b.astype(BF16), preferred_element_type=F32)


INVERSE_EXACT_LEVELS = 1


def _neumann_inverses(ms, eye):
    c = CHUNK
    ps = [eye + m for m in ms]
    xs = [_mm3(m, m) for m in ms]
    levels = c.bit_length() - 2
    for lvl in range(levels):
        mm = _mm3 if lvl < INVERSE_EXACT_LEVELS else _mm1
        if lvl == levels - 1:
            ps = [p + mm(p, x) for p, x in zip(ps, xs)]
        else:
            rs = [mm(jnp.concatenate([x, p], axis=0), x) for p, x in zip(ps, xs)]
            xs = [r[:c] for r in rs]
            ps = [p + r[c:] for p, r in zip(ps, rs)]
    return ps


def _dn_scan_kernel(qf_ref, kf_ref, vf_ref, qb_ref, kb_ref, vb_ref, beta_f_ref, ld_f_ref,
                    beta_b_ref, ld_b_ref, of_ref, ob_ref, st_ref, *, gt, rep):
    c = CHUNK
    nch = gt // c

    @pl.when(pl.program_id(2) == 0)
    def _():
        st_ref[...] = jnp.zeros_like(st_ref)

    row = lax.broadcasted_iota(jnp.int32, (c, c), 0)
    col = lax.broadcasted_iota(jnp.int32, (c, c), 1)
    diag = row == col
    eye = jnp.where(diag, 1.0, 0.0)
    incl = (col <= row, col >= row)
    strict = (col < row, col > row)
    end = (c - 1, 0)
    q_refs = (qf_ref, qb_ref)
    k_refs = (kf_ref, kb_ref)
    v_refs = (vf_ref, vb_ref)
    o_refs = (of_ref, ob_ref)
    beta_refs = (beta_f_ref, beta_b_ref)
    ld_refs = (ld_f_ref, ld_b_ref)

    def to_col(x_row):
        return jnp.sum(jnp.where(diag, x_row, 0.0), axis=1, keepdims=True)

    units = {}
    for d in range(2):
        for ci in range(nch):
            rows = slice(ci * c, (ci + 1) * c)
            q = q_refs[d][rows, :]
            k = k_refs[d][rows, :]
            qb16 = q.astype(BF16)
            kb16 = k.astype(BF16)
            kk = _dot_nt(kb16, kb16)
            qk = _dot_nt(qb16, kb16)
            for r in range(rep):
                beta_row = beta_refs[d][r, ci:ci + 1, :]
                g_row = ld_refs[d][r, ci:ci + 1, :]
                g_col = to_col(g_row)
                beta_col = to_col(beta_row)
                gc_col = jnp.sum(jnp.where(incl[d], g_row, 0.0), axis=1, keepdims=True)
                gc_row = jnp.sum(jnp.where(incl[1 - d], g_col, 0.0), axis=0, keepdims=True)
                dec = jnp.exp(jnp.minimum(gc_col - gc_row, 0.0))
                gc_end = gc_col[end[d]:end[d] + 1, :]
                eg = jnp.exp(gc_col)
                v = v_refs[d][rows, r * HEAD_DIM:(r + 1) * HEAD_DIM]
                units[d, ci, r] = dict(
                    m=jnp.where(strict[d], -(beta_col * kk * dec), 0.0),
                    rhs_v=(beta_col * v).astype(BF16),
                    rhs_k=(beta_col * eg * k).astype(BF16),
                    a=jnp.where(incl[d], qk * dec, 0.0).astype(BF16),
                    kd_t=(k * jnp.exp(gc_end - gc_col)).T.astype(BF16),
                    eg=eg, dec_end=jnp.exp(gc_end), qb16=qb16)

    keys = list(units)
    for key, tinv in zip(keys, _neumann_inverses([units[key]["m"] for key in keys], eye)):
        u = units[key]
        sol = jnp.dot(tinv.astype(BF16), jnp.concatenate([u["rhs_v"], u["rhs_k"]], axis=1),
                      preferred_element_type=F32)
        u["sol_v"] = sol[:, :HEAD_DIM]
        u["kq"] = jnp.concatenate([sol[:, HEAD_DIM:].astype(BF16), u["qb16"]], axis=0)
        u["a_kd"] = jnp.concatenate([u["a"], u["kd_t"]], axis=0)

    chains = [(d, r) for d in range(2) for r in range(rep)]
    states = {ch: st_ref[ch[0], ch[1]] for ch in chains}
    for step in range(nch):
        for d, r in chains:
            ci = step if d == 0 else nch - 1 - step
            u = units[d, ci, r]
            st = states[d, r]
            ks_qs = jnp.dot(u["kq"], st.astype(BF16), preferred_element_type=F32)
            w16 = (u["sol_v"] - ks_qs[:c]).astype(BF16)
            aw_kw = jnp.dot(u["a_kd"], w16, preferred_element_type=F32)
            o_refs[d][ci * c:(ci + 1) * c, r * HEAD_DIM:(r + 1) * HEAD_DIM] = u["eg"] * ks_qs[c:] + aw_kw[:c]
            states[d, r] = st * u["dec_end"] + aw_kw[c:]
    for d, r in chains:
        st_ref[d, r] = states[d, r]


def _dn_scan_call(qkv, beta, ld, nb, t, gt, key_w, rep):
    ng = t // gt
    hd = HEAD_DIM
    hq = key_w // hd
    nch = gt // CHUNK
    koff = key_w // hd
    voff = 2 * key_w // (rep * hd)

    def spec(width, off, bwd):
        if bwd:
            return pl.BlockSpec((None, gt, width), lambda b, h, s: (b, _bwd_group(s, ng), off + h))
        return pl.BlockSpec((None, gt, width), lambda b, h, s: (b, s, off + h))

    def sspec(d):
        if d == 1:
            return pl.BlockSpec((None, None, rep, None, nch, CHUNK),
                                lambda b, h, s: (1, b, h, _bwd_group(s, ng), 0, 0))
        return pl.BlockSpec((None, None, rep, None, nch, CHUNK), lambda b, h, s: (0, b, h, s, 0, 0))

    in_specs, args = [], []
    for d in range(2):
        in_specs += [spec(hd, 0, d == 1), spec(hd, koff, d == 1), spec(rep * hd, voff, d == 1)]
        args += [qkv, qkv, qkv]
    in_specs += [sspec(0), sspec(0), sspec(1), sspec(1)]
    args += [beta, ld, beta, ld]
    out = jax.ShapeDtypeStruct((nb, t, hq * rep * hd), F32)
    kern = functools.partial(_dn_scan_kernel, gt=gt, rep=rep)
    return pl.pallas_call(
        kern,
        out_shape=(out, out),
        grid=(nb, hq, ng),
        in_specs=in_specs,
        out_specs=(pl.BlockSpec((None, gt, rep * hd), lambda b, h, s: (b, s, h)),
                   pl.BlockSpec((None, gt, rep * hd), lambda b, h, s: (b, _bwd_group(s, ng), h))),
        scratch_shapes=[pltpu.VMEM((2, rep, hd, hd), F32)],
        compiler_params=_params(("parallel", "parallel", "arbitrary")),
        name="deltanet_scan",
    )(*args)


def _final_norm_kernel(h_ref, w_ref, o_ref):
    x = h_ref[...]
    o_ref[...] = x * lax.rsqrt(jnp.mean(x * x, axis=-1, keepdims=True) + EPS) * w_ref[...]


def _final_norm_call(h3, w, ctx, seq):
    nb, t, d = h3.shape
    skip = ctx // ctx
    return pl.pallas_call(
        _final_norm_kernel,
        out_shape=jax.ShapeDtypeStruct((nb, seq, d), F32),
        grid=(nb, seq // ctx),
        in_specs=[pl.BlockSpec((None, ctx, d), lambda b, i: (b, i + skip, 0)),
                  pl.BlockSpec((1, d), lambda b, i: (0, 0))],
        out_specs=pl.BlockSpec((None, ctx, d), lambda b, i: (b, i, 0)),
        compiler_params=_params(("parallel", "parallel")),
        name="final_norm",
    )(h3, w.reshape(1, d))


def _scan_scalars(x, nb, t, gt):
    hv = x.shape[-1]
    x = x.reshape(2, nb, t, hv).transpose(0, 1, 3, 2)
    return x.reshape(2, nb, hv, t // gt, gt // CHUNK, CHUNK)


def kernel(x, c, ctx, c_ctx, w_ada, b_ada, norm_mix, norm_ffn, hg_lb_logits, hg_w_in, hg_w_f, hg_o_norm, hg_w_o, gla_w_in, gla_w_g1, gla_w_g2, gla_b_g, gla_o_norm, gla_w_o, dn_w_in, dn_conv, dn_w_b, dn_w_a, dn_a_log, dn_dt_bias, dn_o_norm, dn_w_o, ffn_w_gate, ffn_w_up, ffn_w_down, final_norm):
    nb, seq, d = x.shape
    n_ctx = ctx.shape[1]
    depth = w_ada.shape[0]
    t = n_ctx + seq
    gt = n_ctx
    assert n_ctx % CHUNK == 0 and seq % gt == 0 and GRID_W == CHUNK and nb + 1 <= MOD_ROWS
    dims = (t, n_ctx, nb)
    m = nb * t

    (hg_w_in, hg_w_f, hg_w_o, gla_w_in, gla_w_g1, gla_w_g2, gla_w_o, dn_w_in, dn_w_b, dn_w_a, dn_w_o,
     ffn_w_gate, ffn_w_up, ffn_w_down) = (
        w.astype(BF16) for w in (hg_w_in, hg_w_f, hg_w_o, gla_w_in, gla_w_g1, gla_w_g2, gla_w_o, dn_w_in,
                                 dn_w_b, dn_w_a, dn_w_o, ffn_w_gate, ffn_w_up, ffn_w_down))

    p = jax.nn.softmax(hg_lb_logits.astype(F32), axis=0)
    lower_bounds = jnp.cumsum(p, axis=0) - p[0]

    cc = jnp.zeros((MOD_ROWS, d), F32).at[:nb].set(c).at[nb].set(c_ctx)
    mods = _ada_call(cc, w_ada, b_ada)

    h = jnp.concatenate([ctx, x], axis=1).reshape(m, d)

    hg_heads = d // HEAD_DIM
    hg_kw = hg_w_f.shape[-1]
    gla_kw = gla_w_g2.shape[-1]
    gla_dk = gla_kw // GLA_HEADS
    gla_dv = d // GLA_HEADS
    dn_key_w = d
    dn_hv = dn_w_b.shape[-1]
    dn_val_w = dn_hv * HEAD_DIM
    dn_qkv_w = 2 * dn_key_w + dn_val_w
    rep = dn_hv // (dn_key_w // HEAD_DIM)

    for i in range(depth):
        kind, j = i % 3, i // 3
        src = _NormSrc(h, norm_mix.reshape(depth, 1, d), mods, i, 0, 1, dims)
        if kind == 0:
            proj = _proj_call(lambda acc: _silu(acc) * HEAD_DIM ** -0.5, src, hg_w_in, j,
                              "hgrn2_in_proj", hg_kw)
            kk, lf = _hg_f_call(src, hg_w_f, j, lower_bounds[i])
            p3 = proj.reshape(nb, t, -1)
            kk4 = kk.reshape(2, nb, t, hg_kw)
            o_f, o_b = _gla_scan_call((p3, 0), ((kk4, 0, 0), (kk4, 1, 0)), (p3, hg_kw),
                                      lf.reshape(2, nb, t, hg_kw), nb, t, gt, hg_heads, HEAD_DIM,
                                      d // hg_heads)
            h = _gated_out_call(o_f.reshape(m, -1), o_b.reshape(m, -1), proj, hg_kw + d, hg_o_norm, hg_w_o, j,
                                h, mods, i, 2, dims, "hgrn2_out_proj")
        elif kind == 1:
            proj = _proj_call(lambda acc: acc * gla_dk ** -0.5, src, gla_w_in, j,
                              "gla_in_proj", gla_kw)
            lg = _gla_gate_call(src, gla_w_g1, gla_w_g2, gla_b_g, j)
            p3 = proj.reshape(nb, t, -1)
            o_f, o_b = _gla_scan_call((p3, 0), ((p3, None, gla_kw), (p3, None, gla_kw)), (p3, 2 * gla_kw),
                                      lg.reshape(2, nb, t, gla_kw), nb, t, gt, GLA_HEADS, gla_dk, gla_dv)
            h = _gated_out_call(o_f.reshape(m, -1), o_b.reshape(m, -1), proj, 2 * gla_kw + d, gla_o_norm,
                                gla_w_o, j, h, mods, i, 2, dims, "gla_out_proj")
        else:
            proj = _proj_call(None, src, dn_w_in, j, "deltanet_in_proj")
            beta, ld = _dn_bg_call(src, dn_w_b, dn_w_a, dn_a_log, dn_dt_bias, j)
            qkv = _dn_conv_call(proj.reshape(nb, t, -1), dn_conv, j, nb, t, gt, dn_qkv_w, dn_key_w)
            o_f, o_b = _dn_scan_call(qkv, _scan_scalars(beta, nb, t, gt), _scan_scalars(ld, nb, t, gt),
                                     nb, t, gt, dn_key_w, rep)
            h = _gated_out_call(o_f.reshape(m, -1), o_b.reshape(m, -1), proj, dn_qkv_w, dn_o_norm, dn_w_o, j,
                                h, mods, i, 2, dims, "deltanet_out_proj")
        src = _NormSrc(h, norm_ffn.reshape(depth, 1, d), mods, i, 3, 4, dims)
        u = _swiglu_call(src, ffn_w_gate, ffn_w_up, i)
        h = _residual_call(u, ffn_w_down, i, h, mods, i, 5, dims, "ffn_down_proj")

    return _final_norm_call(h.reshape(nb, t, d), final_norm, n_ctx, seq)
```
